```python
import math
import jax, jax.numpy as jnp
from jax import lax
import numpy as np

D_MODEL = 1024
BATCH = 2
SEQ = 8192
DEPTH = 2

HEAD_DIM = 64
N_HEADS_DIFF = 4
N_HEADS_DIL = 6
N_HEADS_MOBA = 6
N_HEADS_MIX = N_HEADS_DIFF + N_HEADS_DIL + N_HEADS_MOBA
MIX_WIDTH = N_HEADS_MIX * HEAD_DIM
DIFF_HALF = HEAD_DIM // 2
ROPE_THETA = 500000.0
ROPE_FRACTION = 4
DILATED_PAIRS = ((128, 1), (512, 4), (2048, 16))
MOBA_BLOCK = 256
MOBA_TOPK = 3
MOBA_Q_CHUNK = 64
Q_BLOCK = 128
MEM_LEN = 256
N_HEADS_MEM = 4
MEM_WIDTH = N_HEADS_MEM * HEAD_DIM
D_FF = -(-(8 * D_MODEL) // (3 * 256)) * 256
EPS = 1e-6

kernel_name = "hymba_style_diff_dilated_moba_hybrid"


def rms_norm(x, w):
    xf = x.astype(jnp.float32)
    y = xf * lax.rsqrt(jnp.mean(xf * xf, axis=-1, keepdims=True) + EPS)
    return (y * w.astype(jnp.float32)).astype(x.dtype)


def rotary(x, positions, rot_dim):
    half = rot_dim // 2
    inv_freq = ROPE_THETA ** (-jnp.arange(half, dtype=jnp.float32) / half)
    ang = positions.astype(jnp.float32)[..., None] * inv_freq
    cos = jnp.cos(ang)[:, :, None, :]
    sin = jnp.sin(ang)[:, :, None, :]
    xr = x[..., :rot_dim].astype(jnp.float32)
    x1, x2 = xr[..., :half], xr[..., half:]
    rot = jnp.concatenate([x1 * cos - x2 * sin, x2 * cos + x1 * sin], axis=-1).astype(x.dtype)
    return jnp.concatenate([rot, x[..., rot_dim:]], axis=-1)


def diff_qk_prep(t, gain, positions):
    B, S, H, _ = t.shape
    t = t.reshape(B, S, 2 * H, DIFF_HALF)
    t = rotary(rms_norm(t, gain), positions, DIFF_HALF // ROPE_FRACTION)
    return t.reshape(B, S, H, 2, DIFF_HALF)


def diff_attention(q1, q2, k1, k2, v, lam):
    B, S, H, Dh = q1.shape
    scale = Dh ** -0.5
    key_idx = jnp.arange(S)

    def one_block(start):
        qs1 = lax.dynamic_slice_in_dim(q1, start, Q_BLOCK, axis=1)
        qs2 = lax.dynamic_slice_in_dim(q2, start, Q_BLOCK, axis=1)
        q_idx = start + jnp.arange(Q_BLOCK)
        mask = (key_idx[None, :] <= q_idx[:, None])[None, None]
        s1 = jnp.einsum('bqhd,bkhd->bhqk', qs1, k1, preferred_element_type=jnp.float32) * scale
        s2 = jnp.einsum('bqhd,bkhd->bhqk', qs2, k2, preferred_element_type=jnp.float32) * scale
        p1 = jax.nn.softmax(jnp.where(mask, s1, -jnp.inf), axis=-1)
        p2 = jax.nn.softmax(jnp.where(mask, s2, -jnp.inf), axis=-1)
        p = (p1 - lam * p2).astype(v.dtype)
        return jnp.einsum('bhqk,bkhd->bqhd', p, v)

    out = lax.map(one_block, jnp.arange(S // Q_BLOCK) * Q_BLOCK)
    return jnp.moveaxis(out, 0, 1).reshape(B, S, H, v.shape[-1])


def dilated_branch(q, k, v, window, dilation):
    B, S, H, D = q.shape
    W = window // dilation
    unit = dilation * W
    Sp = -(-S // unit) * unit
    nc = Sp // unit
    pad = Sp - S

    def to_blocks(t):
        t = jnp.pad(t, ((0, 0), (0, pad), (0, 0), (0, 0)))
        return t.reshape(B, nc, W, dilation, H, D)

    def with_prev(t):
        prev = jnp.pad(t, ((0, 0), (1, 0), (0, 0), (0, 0), (0, 0), (0, 0)))[:, :-1]
        return jnp.concatenate([prev, t], axis=2)

    qb = to_blocks(q)
    kw = with_prev(to_blocks(k))
    vw = with_prev(to_blocks(v))
    s = jnp.einsum('bcirhd,bcjrhd->bcrhij', qb, kw, preferred_element_type=jnp.float32) * (D ** -0.5)
    m_q = jnp.arange(nc)[:, None, None] * W + jnp.arange(W)[None, :, None]
    m_k = jnp.arange(nc)[:, None, None] * W + jnp.arange(2 * W)[None, None, :] - W
    delta = m_q - m_k
    valid = (delta >= 0) & (delta <= W) & (m_k >= 0)
    s = jnp.where(valid[None, :, None, None, :, :], s, -jnp.inf)
    lse = jax.nn.logsumexp(s, axis=-1)
    p = jnp.exp(s - lse[..., None]).astype(v.dtype)
    o = jnp.einsum('bcrhij,bcjrhd->bcirhd', p, vw).reshape(B, Sp, H, D)[:, :S]
    lse = jnp.transpose(lse, (0, 1, 4, 2, 3)).reshape(B, Sp, H)[:, :S]
    return o, lse


def dilated_mixture(q, k, v):
    outs, lses = [], []
    for window, dilation in DILATED_PAIRS:
        o, lse = dilated_branch(q, k, v, window, dilation)
        outs.append(o)
        lses.append(lse)
    w = jax.nn.softmax(jnp.stack(lses, axis=0), axis=0)
    o = jnp.sum(w[..., None] * jnp.stack(outs, axis=0).astype(jnp.float32), axis=0)
    return o.astype(v.dtype)


def moba_attention(q, k, v):
    B, S, H, D = q.shape
    scale = D ** -0.5
    Sp = -(-S // MOBA_BLOCK) * MOBA_BLOCK
    nblk = Sp // MOBA_BLOCK
    n_sel = min(MOBA_TOPK, nblk)
    pad = Sp - S
    kp = jnp.pad(k, ((0, 0), (0, pad), (0, 0), (0, 0)))
    vp = jnp.pad(v, ((0, 0), (0, pad), (0, 0), (0, 0)))
    kbt = kp.reshape(B, nblk, MOBA_BLOCK, H, D).transpose(0, 1, 3, 2, 4)
    vbt = vp.reshape(B, nblk, MOBA_BLOCK, H, D).transpose(0, 1, 3, 2, 4)
    k_mean = jnp.mean(kbt.astype(jnp.float32), axis=3)
    b_idx = jnp.arange(B)[:, None, None, None]
    h_idx = jnp.arange(H)[None, None, :, None]
    blk_ids = jnp.arange(nblk)

    def one_chunk(start):
        qs = lax.dynamic_slice_in_dim(q, start, MOBA_Q_CHUNK, axis=1)
        q_pos = start + jnp.arange(MOBA_Q_CHUNK)
        own = start // MOBA_BLOCK
        gate = jnp.einsum('bqhd,bnhd->bqhn', qs.astype(jnp.float32), k_mean)
        gate = jnp.where(blk_ids < own, gate, -jnp.inf)
        _, sel = lax.top_k(gate, n_sel)
        sel_valid = sel < own
        ks = kbt[b_idx, sel, h_idx]
        vs = vbt[b_idx, sel, h_idx]
        s_sel = jnp.einsum('bqhd,bqhnkd->bqhnk', qs, ks, preferred_element_type=jnp.float32) * scale
        s_sel = jnp.where(sel_valid[..., None], s_sel, -jnp.inf).reshape(B, MOBA_Q_CHUNK, H, n_sel * MOBA_BLOCK)
        k_own = lax.dynamic_slice_in_dim(kp, own * MOBA_BLOCK, MOBA_BLOCK, axis=1)
        v_own = lax.dynamic_slice_in_dim(vp, own * MOBA_BLOCK, MOBA_BLOCK, axis=1)
        own_pos = own * MOBA_BLOCK + jnp.arange(MOBA_BLOCK)
        s_own = jnp.einsum('bqhd,bkhd->bqhk', qs, k_own, preferred_element_type=jnp.float32) * scale
        s_own = jnp.where((own_pos[None, :] <= q_pos[:, None])[None, :, None, :], s_own, -jnp.inf)
        p = jax.nn.softmax(jnp.concatenate([s_sel, s_own], axis=-1), axis=-1).astype(v.dtype)
        p_sel = p[..., :n_sel * MOBA_BLOCK].reshape(B, MOBA_Q_CHUNK, H, n_sel, MOBA_BLOCK)
        p_own = p[..., n_sel * MOBA_BLOCK:]
        return (jnp.einsum('bqhnk,bqhnkd->bqhd', p_sel, vs)
                + jnp.einsum('bqhk,bkhd->bqhd', p_own, v_own))

    out = lax.map(one_chunk, jnp.arange(S // MOBA_Q_CHUNK) * MOBA_Q_CHUNK)
    return jnp.moveaxis(out, 0, 1).reshape(B, S, H, D)


def memory_cross_attention(h, mem_n, w_q, w_kv, w_o, q_gain, k_gain):
    B, S, _ = h.shape
    M = mem_n.shape[1]
    q = rms_norm((h @ w_q).reshape(B, S, N_HEADS_MEM, HEAD_DIM), q_gain)
    kv = (mem_n @ w_kv).reshape(B, M, 2, N_HEADS_MEM, HEAD_DIM)
    k = rms_norm(kv[:, :, 0], k_gain)
    v = kv[:, :, 1]
    s = jnp.einsum('bshd,bmhd->bhsm', q, k, preferred_element_type=jnp.float32) * (HEAD_DIM ** -0.5)
    p = jax.nn.softmax(s, axis=-1).astype(v.dtype)
    o = jnp.einsum('bhsm,bmhd->bshd', p, v).reshape(B, S, MEM_WIDTH)
    return o @ w_o


def swiglu(h, w_gate_up, w_down):
    gu = h @ w_gate_up
    g, u = gu[..., :D_FF], gu[..., D_FF:]
    return (jax.nn.silu(g) * u) @ w_down


def setup_inputs(seed: int = 0) -> dict:
    key = jax.random.key(seed)
    ks = iter(jax.random.split(key, 40))

    def nrm(shape, scale):
        return jax.random.normal(next(ks), shape, dtype=jnp.float32) * scale

    def gain(shape):
        return 1.0 + nrm(shape, 0.02)

    L = DEPTH
    return {
        "x": nrm((BATCH, SEQ, D_MODEL), 1.0),
        "mem": nrm((BATCH, MEM_LEN, D_MODEL), 1.0),
        "positions": jnp.broadcast_to(jnp.arange(SEQ, dtype=jnp.int32)[None, :], (BATCH, SEQ)),
        "norm_mix": gain((L, D_MODEL)),
        "w_in": nrm((L, D_MODEL, 3 * MIX_WIDTH), D_MODEL ** -0.5),
        "qn_diff": gain((L, DIFF_HALF)),
        "kn_diff": gain((L, DIFF_HALF)),
        "lambda_q1": nrm((L, DIFF_HALF), 0.1),
        "lambda_k1": nrm((L, DIFF_HALF), 0.1),
        "lambda_q2": nrm((L, DIFF_HALF), 0.1),
        "lambda_k2": nrm((L, DIFF_HALF), 0.1),
        "subln_diff": gain((L, HEAD_DIM)),
        "qn_dil": gain((L, HEAD_DIM)),
        "kn_dil": gain((L, HEAD_DIM)),
        "qn_moba": gain((L, HEAD_DIM)),
        "kn_moba": gain((L, HEAD_DIM)),
        "w_out": nrm((L, MIX_WIDTH, D_MODEL), MIX_WIDTH ** -0.5),
        "norm_cross": gain((L, D_MODEL)),
        "norm_mem": gain((L, D_MODEL)),
        "w_mq": nrm((L, D_MODEL, MEM_WIDTH), D_MODEL ** -0.5),
        "w_mkv": nrm((L, D_MODEL, 2 * MEM_WIDTH), D_MODEL ** -0.5),
        "qn_mem": gain((L, HEAD_DIM)),
        "kn_mem": gain((L, HEAD_DIM)),
        "w_mo": nrm((L, MEM_WIDTH, D_MODEL), MEM_WIDTH ** -0.5),
        "norm_ffn": gain((L, D_MODEL)),
        "w_gate_up": nrm((L, D_MODEL, 2 * D_FF), D_MODEL ** -0.5),
        "w_down": nrm((L, D_FF, D_MODEL), D_FF ** -0.5),
    }


def reference(x, mem, positions, norm_mix, w_in, qn_diff, kn_diff, lambda_q1, lambda_k1,
              lambda_q2, lambda_k2, subln_diff, qn_dil, kn_dil, qn_moba, kn_moba, w_out,
              norm_cross, norm_mem, w_mq, w_mkv, qn_mem, kn_mem, w_mo, norm_ffn,
              w_gate_up, w_down):
    B, S, _ = x.shape
    a_end = N_HEADS_DIFF
    b_end = N_HEADS_DIFF + N_HEADS_DIL
    rot = HEAD_DIM // ROPE_FRACTION
    for l in range(DEPTH):
        h = rms_norm(x, norm_mix[l])
        qkv = (h @ w_in[l]).reshape(B, S, 3, N_HEADS_MIX, HEAD_DIM)
        q, k, v = qkv[:, :, 0], qkv[:, :, 1], qkv[:, :, 2]

        lam_init = 0.8 - 0.6 * math.exp(-0.3 * l)
        lam = (jnp.exp(jnp.sum(lambda_q1[l].astype(jnp.float32) * lambda_k1[l].astype(jnp.float32)))
               - jnp.exp(jnp.sum(lambda_q2[l].astype(jnp.float32) * lambda_k2[l].astype(jnp.float32)))
               + lam_init)
        qa = diff_qk_prep(q[:, :, :a_end], qn_diff[l], positions)
        ka = diff_qk_prep(k[:, :, :a_end], kn_diff[l], positions)
        o_a = diff_attention(qa[:, :, :, 0], qa[:, :, :, 1], ka[:, :, :, 0], ka[:, :, :, 1],
                             v[:, :, :a_end], lam)
        o_a = rms_norm(o_a, subln_diff[l]) * (1.0 - lam_init)

        qb = rotary(rms_norm(q[:, :, a_end:b_end], qn_dil[l]), positions, rot)
        kb = rotary(rms_norm(k[:, :, a_end:b_end], kn_dil[l]), positions, rot)
        o_b = dilated_mixture(qb, kb, v[:, :, a_end:b_end])

        qc = rotary(rms_norm(q[:, :, b_end:], qn_moba[l]), positions, rot)
        kc = rotary(rms_norm(k[:, :, b_end:], kn_moba[l]), positions, rot)
        o_c = moba_attention(qc, kc, v[:, :, b_end:])

        mix = jnp.concatenate([o_a, o_b.astype(o_a.dtype), o_c.astype(o_a.dtype)], axis=2)
        x = x + mix.reshape(B, S, MIX_WIDTH) @ w_out[l]

        x = x + memory_cross_attention(rms_norm(x, norm_cross[l]), rms_norm(mem, norm_mem[l]),
                                       w_mq[l], w_mkv[l], w_mo[l], qn_mem[l], kn_mem[l])

        x = x + swiglu(rms_norm(x, norm_ffn[l]), w_gate_up[l], w_down[l])
    return x
```

```python
import functools
import math

import jax
import jax.numpy as jnp
from jax import lax
from jax.experimental import pallas as pl
from jax.experimental.pallas import tpu as pltpu

D_MODEL = 1024
HEAD_DIM = 64
N_HEADS_DIFF = 4
N_HEADS_DIL = 6
N_HEADS_MOBA = 6
DIFF_HALF = HEAD_DIM // 2
ROPE_THETA = 500000.0
ROPE_FRACTION = 4
DILATED_PAIRS = ((128, 1), (512, 4), (2048, 16))
DIL_STEPS = 128
MOBA_BLOCK = 256
MOBA_TOPK = 3
MEM_LEN = 256
N_HEADS_MEM = 4
MEM_WIDTH = N_HEADS_MEM * HEAD_DIM
D_FF = 2816
EPS = 1e-6

W_DIFF = N_HEADS_DIFF * HEAD_DIM
W_DIL = N_HEADS_DIL * HEAD_DIM
W_MOBA = N_HEADS_MOBA * HEAD_DIM
MIX_WIDTH = W_DIFF + W_DIL + W_MOBA

LANES = 128
MXU_DIM = 256
VMEM_LIMIT_BYTES = 56 * 2**20

MASKED = -1e30

BF16 = jnp.bfloat16
F32 = jnp.float32
NT_DIMS = (((1,), (1,)), ((), ()))


def _dot(a, b):
    return jnp.dot(a, b, preferred_element_type=F32)


def _dot_nt(a, b):
    return lax.dot_general(a, b, NT_DIMS, preferred_element_type=F32)


def _params(n_grid):
    return pltpu.CompilerParams(dimension_semantics=("arbitrary",) * n_grid,
                                vmem_limit_bytes=VMEM_LIMIT_BYTES)


def _resident(shape):
    nd = len(shape)
    return pl.BlockSpec(shape, lambda *_: (0,) * nd, pipeline_mode=pl.Buffered(1))


def _rms(x, w):
    return x * lax.rsqrt(jnp.mean(x * x, axis=-1, keepdims=True) + EPS) * w


def _lane_iota(n=LANES):
    return lax.broadcasted_iota(jnp.int32, (1, n), 1)


def _rope_kernel(pos_ref, freq_ref, ch_ref, sh_ref, cd_ref, sd_ref):
    pos = pos_ref[...].astype(F32)
    lane = _lane_iota()
    for row, period, c_ref, s_ref in ((0, HEAD_DIM, ch_ref, sh_ref), (1, DIFF_HALF, cd_ref, sd_ref)):
        rot = period // ROPE_FRACTION
        within = lane & (period - 1)
        ang = pos * freq_ref[row:row + 1, :]
        sin = jnp.sin(ang)
        c_ref[...] = jnp.where(within < rot, jnp.cos(ang), 1.0)
        s_ref[...] = jnp.where(within < rot // 2, -sin, jnp.where(within < rot, sin, 0.0))


def _rope_tables(positions):
    rows = positions.size
    tm = 2048
    lane = jnp.arange(LANES)
    freqs = []
    for period in (HEAD_DIM, DIFF_HALF):
        half = period // ROPE_FRACTION // 2
        inv_freq = ROPE_THETA ** (-jnp.arange(half, dtype=F32) / half)
        freqs.append(inv_freq[(lane % period) % half])
    table = jax.ShapeDtypeStruct((rows, LANES), F32)
    return pl.pallas_call(
        _rope_kernel,
        grid=(rows // tm,),
        in_specs=[pl.BlockSpec((tm, 1), lambda i: (i, 0)), pl.BlockSpec((2, LANES), lambda i: (0, 0))],
        out_specs=[pl.BlockSpec((tm, LANES), lambda i: (i, 0))] * 4,
        out_shape=[table] * 4,
        compiler_params=_params(1),
        name="rope_tables",
    )(positions.reshape(rows, 1), jnp.stack(freqs))


def _group_mean_matrix(group):
    idx = jnp.arange(MXU_DIM) // group
    return jnp.where(idx[:, None] == idx[None, :], 1.0 / group, 0.0).astype(BF16)


def _rotate(t, cos, sin, half, period):
    first = (_lane_iota() & (period - 1)) < half
    partner = jnp.where(first, pltpu.roll(t, LANES - half, 1), pltpu.roll(t, half, 1))
    return t * cos + partner * sin


def _qkv_kernel(x_ref, nw_ref, w_ref, gain_ref, g32_ref, g64_ref, ch_ref, sh_ref, cd_ref, sd_ref,
                a_ref, b_ref, c_ref):
    h = _rms(x_ref[...], nw_ref[...]).astype(BF16)
    n_chunks = MIX_WIDTH // MXU_DIM
    for chunk in range(3 * n_chunks):
        section, col0 = divmod(chunk * MXU_DIM, MIX_WIDTH)
        y = _dot(h, w_ref[:, chunk * MXU_DIM:(chunk + 1) * MXU_DIM])
        if section < 2:
            is_diff = col0 < W_DIFF
            g_ref = g32_ref if is_diff else g64_ref
            ms = _dot((y * y).astype(BF16), g_ref[...])
            y = y * lax.rsqrt(ms + EPS) * gain_ref[:, chunk * MXU_DIM:(chunk + 1) * MXU_DIM]
        for piece in range(MXU_DIM // LANES):
            col = col0 + piece * LANES
            t = y[:, piece * LANES:(piece + 1) * LANES]
            if section < 2:
                if col < W_DIFF:
                    t = _rotate(t, cd_ref[...], sd_ref[...], DIFF_HALF // ROPE_FRACTION // 2, DIFF_HALF)
                else:
                    t = _rotate(t, ch_ref[...], sh_ref[...], HEAD_DIM // ROPE_FRACTION // 2, HEAD_DIM)
            if col < W_DIFF:
                dst = section * W_DIFF + col
                a_ref[:, dst:dst + LANES] = t.astype(BF16)
            elif col < W_DIFF + W_DIL:
                dst = section * W_DIL + col - W_DIFF
                b_ref[:, dst:dst + LANES] = t
            else:
                dst = section * W_MOBA + col - W_DIFF - W_DIL
                c_ref[:, dst:dst + LANES] = t.astype(BF16)


def _qkv_project(x2d, norm_w, w_in, gains, tables, tm=512):
    rows = x2d.shape[0]
    row_spec = lambda width: pl.BlockSpec((tm, width), lambda i: (i, 0))
    return pl.pallas_call(
        _qkv_kernel,
        grid=(rows // tm,),
        in_specs=[row_spec(D_MODEL), _resident((1, D_MODEL)), _resident((D_MODEL, 3 * MIX_WIDTH)),
                  _resident((1, 2 * MIX_WIDTH)), _resident((MXU_DIM, MXU_DIM)), _resident((MXU_DIM, MXU_DIM))]
                 + [row_spec(LANES)] * 4,
        out_specs=[row_spec(3 * W_DIFF), row_spec(3 * W_DIL), row_spec(3 * W_MOBA)],
        out_shape=[jax.ShapeDtypeStruct((rows, 3 * W_DIFF), BF16),
                   jax.ShapeDtypeStruct((rows, 3 * W_DIL), F32),
                   jax.ShapeDtypeStruct((rows, 3 * W_MOBA), BF16)],
        compiler_params=_params(1),
        name="qkv_project",
    )(x2d, norm_w.reshape(1, D_MODEL), w_in.astype(BF16), gains,
      _group_mean_matrix(DIFF_HALF), _group_mean_matrix(HEAD_DIM), *tables)


def _online_softmax_step(s, v, m_ref, l_ref, acc_ref):
    m_old = m_ref[...]
    m_new = jnp.maximum(m_old, jnp.max(s, axis=-1, keepdims=True))
    alpha = jnp.exp(m_old - m_new)
    e = jnp.exp(s - m_new)
    l_ref[...] = alpha * l_ref[...] + jnp.sum(e, axis=-1, keepdims=True)
    acc_ref[...] = alpha * acc_ref[...] + _dot(e.astype(BF16), v)
    m_ref[...] = m_new


def _diff_kernel(lq1_ref, lk1_ref, lq2_ref, lk2_ref, subln_ref, q_ref, k_ref, v_ref, o_ref,
                 q4_ref, m_ref, l_ref, acc_ref, *, tq, tk, lam_init):
    i = pl.program_id(2)
    lane = _lane_iota()
    q = q_ref[0]
    for n in range(4):
        keep = (lane >= n * DIFF_HALF) & (lane < (n + 1) * DIFF_HALF)
        q4_ref[n * tq:(n + 1) * tq, :] = jnp.where(keep, q, jnp.zeros_like(q))
    m_ref[...] = jnp.full(m_ref.shape, MASKED, F32)
    l_ref[...] = jnp.zeros(l_ref.shape, F32)
    acc_ref[...] = jnp.zeros(acc_ref.shape, F32)

    def scores(j):
        start = pl.multiple_of(j * tk, tk)
        s = _dot_nt(q4_ref[...], k_ref[0, pl.ds(start, tk), :])
        return s, v_ref[0, pl.ds(start, tk), :]

    def full_chunk(j, carry):
        s, v = scores(j)
        _online_softmax_step(s, v, m_ref, l_ref, acc_ref)
        return carry

    n_full = (i * tq) // tk
    lax.fori_loop(0, n_full, full_chunk, 0)
    s, v = scores(n_full)
    row = (lax.broadcasted_iota(jnp.int32, s.shape, 0) & (tq - 1)) + i * tq
    col = lax.broadcasted_iota(jnp.int32, s.shape, 1) + n_full * tk
    _online_softmax_step(jnp.where(col <= row, s, -jnp.inf), v, m_ref, l_ref, acc_ref)

    lam = (jnp.exp(jnp.sum(lq1_ref[...] * lk1_ref[...], axis=-1, keepdims=True))
           - jnp.exp(jnp.sum(lq2_ref[...] * lk2_ref[...], axis=-1, keepdims=True)) + lam_init)
    o4 = acc_ref[...] / l_ref[...]
    o = jnp.where(lane < HEAD_DIM, o4[0:tq] - lam * o4[tq:2 * tq], o4[2 * tq:3 * tq] - lam * o4[3 * tq:4 * tq])
    sq = o * o
    ms0 = jnp.sum(jnp.where(lane < HEAD_DIM, sq, 0.0), axis=-1, keepdims=True) * (1.0 / HEAD_DIM)
    ms1 = jnp.sum(jnp.where(lane < HEAD_DIM, 0.0, sq), axis=-1, keepdims=True) * (1.0 / HEAD_DIM)
    ms = jnp.where(lane < HEAD_DIM, ms0, ms1)
    o_ref[0] = (o * lax.rsqrt(ms + EPS) * subln_ref[...] * (1.0 - lam_init)).astype(o_ref.dtype)


def _diff_attention(qkv, lam_vecs, subln, lam_init, tq=256, tk=512):
    batch, seq, _ = qkv.shape
    n_pairs = W_DIFF // LANES
    vec = _resident((1, DIFF_HALF))
    kernel = functools.partial(_diff_kernel, tq=tq, tk=tk, lam_init=lam_init)
    return pl.pallas_call(
        kernel,
        grid=(batch, n_pairs, seq // tq),
        in_specs=[vec, vec, vec, vec, _resident((1, LANES)),
                  pl.BlockSpec((1, tq, LANES), lambda b, p, i: (b, i, p)),
                  pl.BlockSpec((1, seq, LANES), lambda b, p, i: (b, 0, n_pairs + p)),
                  pl.BlockSpec((1, seq, LANES), lambda b, p, i: (b, 0, 2 * n_pairs + p))],
        out_specs=pl.BlockSpec((1, tq, LANES), lambda b, p, i: (b, i, p)),
        out_shape=jax.ShapeDtypeStruct((batch, seq, W_DIFF), BF16),
        scratch_shapes=[pltpu.VMEM((4 * tq, LANES), BF16), pltpu.VMEM((4 * tq, 1), F32),
                        pltpu.VMEM((4 * tq, 1), F32), pltpu.VMEM((4 * tq, LANES), F32)],
        compiler_params=_params(3),
        name="diff_attention",
    )(*[v.reshape(1, DIFF_HALF) for v in lam_vecs], jnp.tile(subln, 2).reshape(1, LANES), qkv, qkv, qkv)


def _dil_kernel(q_ref, k_ref, v_ref, o_ref, o0_ref, o1_ref, o2_ref, l0_ref, l1_ref, l2_ref, *, tile):
    base = pl.program_id(2) * tile
    w = DIL_STEPS
    lane = _lane_iota()
    qi = lax.broadcasted_iota(jnp.int32, (2 * w, 2 * w), 0) & (w - 1)
    kj = lax.broadcasted_iota(jnp.int32, (2 * w, 2 * w), 1)
    band = (kj >= qi) & (kj <= qi + w)

    for (window, d), ob_ref, lse_ref in zip(DILATED_PAIRS, (o0_ref, o1_ref, o2_ref), (l0_ref, l1_ref, l2_ref)):
        def rows(ref, start, d=d):
            if d == 1:
                return ref[0, pl.ds(start, w), :]
            return ref[0, pl.ds(start, w, stride=d), :]

        def block(blk, carry, window=window, d=d, ob_ref=ob_ref, lse_ref=lse_ref, rows=rows):
            local = (blk // d) * window + (blk & (d - 1))
            own = base + local
            has_prev = own >= window
            prev = jnp.where(has_prev, own - window, own)
            q = rows(q_ref, local)
            q2 = jnp.concatenate([jnp.where(lane < HEAD_DIM, q, 0.0), jnp.where(lane < HEAD_DIM, 0.0, q)], axis=0)
            k2 = jnp.concatenate([rows(k_ref, prev), rows(k_ref, own)], axis=0)
            v2 = jnp.concatenate([rows(v_ref, prev), rows(v_ref, own)], axis=0)
            s = _dot_nt(q2.astype(BF16), k2.astype(BF16))
            s = jnp.where(band & (kj >= jnp.where(has_prev, 0, w)), s, -jnp.inf)
            m = jnp.max(s, axis=-1, keepdims=True)
            e = jnp.exp(s - m)
            l = jnp.sum(e, axis=-1, keepdims=True)
            o = _dot(e.astype(BF16), v2.astype(BF16)) / l
            lse = m + jnp.log(l)
            ob = jnp.where(lane < HEAD_DIM, o[0:w], o[w:2 * w])
            lb = jnp.where(lane < HEAD_DIM, lse[0:w], lse[w:2 * w])
            if d == 1:
                ob_ref[pl.ds(local, w), :] = ob
                lse_ref[pl.ds(local, w), :] = lb
            else:
                ob_ref[pl.ds(local, w, stride=d), :] = ob
                lse_ref[pl.ds(local, w, stride=d), :] = lb
            return carry

        lax.fori_loop(0, tile // w, block, 0)

    lse = [l0_ref[...], l1_ref[...], l2_ref[...]]
    m = jnp.maximum(jnp.maximum(lse[0], lse[1]), lse[2])
    wts = [jnp.exp(x - m) for x in lse]
    mixed = wts[0] * o0_ref[...] + wts[1] * o1_ref[...] + wts[2] * o2_ref[...]
    o_ref[0] = (mixed / (wts[0] + wts[1] + wts[2])).astype(o_ref.dtype)


def _dilated_attention(qkv):
    batch, seq, _ = qkv.shape
    n_pairs = W_DIL // LANES
    tile = max(window for window, _ in DILATED_PAIRS)
    assert seq % tile == 0
    return pl.pallas_call(
        functools.partial(_dil_kernel, tile=tile),
        grid=(batch, n_pairs, seq // tile),
        in_specs=[pl.BlockSpec((1, tile, LANES), lambda b, p, i: (b, i, p)),
                  pl.BlockSpec((1, seq, LANES), lambda b, p, i: (b, 0, n_pairs + p)),
                  pl.BlockSpec((1, seq, LANES), lambda b, p, i: (b, 0, 2 * n_pairs + p))],
        out_specs=pl.BlockSpec((1, tile, LANES), lambda b, p, i: (b, i, p)),
        out_shape=jax.ShapeDtypeStruct((batch, seq, W_DIL), BF16),
        scratch_shapes=[pltpu.VMEM((tile, LANES), F32)] * 6,
        compiler_params=_params(3),
        name="dilated_attention",
    )(qkv, qkv, qkv)


def _moba_kernel(q_ref, k_ref, v_ref, o_ref, kaug_ref, kmh_ref, kml_ref, qaug_ref, m_ref, l_ref, acc_ref,
                 *, tk):
    i = pl.program_id(2)
    blk = MOBA_BLOCK
    blk_shift = blk.bit_length() - 1
    seq = k_ref.shape[1]
    lane = _lane_iota()

    @pl.when(i == 0)
    def _():
        piece = 1024
        total = jnp.zeros((LANES, LANES), F32)
        for c in range(seq // piece):
            kc = k_ref[0, c * piece:(c + 1) * piece, :]
            key_blk = (lax.broadcasted_iota(jnp.int32, (piece, LANES), 0) + c * piece) >> blk_shift
            kaug_ref[c * piece:(c + 1) * piece, 0:LANES] = kc
            kaug_ref[c * piece:(c + 1) * piece, LANES:2 * LANES] = (
                lax.broadcasted_iota(jnp.int32, (piece, LANES), 1) == key_blk).astype(BF16)
            member = (lax.broadcasted_iota(jnp.int32, (LANES, piece), 0)
                      == (lax.broadcasted_iota(jnp.int32, (LANES, piece), 1) + c * piece) >> blk_shift)
            total = total + _dot(member.astype(BF16), kc)
        k_mean = total * (1.0 / blk)
        hi = k_mean.astype(BF16)
        kmh_ref[...] = hi
        kml_ref[...] = (k_mean - hi.astype(F32)).astype(BF16)

    q = q_ref[0]
    zero = jnp.zeros_like(q)
    q2 = jnp.concatenate([jnp.where(lane < HEAD_DIM, q, zero), jnp.where(lane < HEAD_DIM, zero, q)], axis=0)

    gate = _dot_nt(q2, kmh_ref[...]) + _dot_nt(q2, kml_ref[...])
    blk_id = lax.broadcasted_iota(jnp.int32, gate.shape, 1)
    blk_f = blk_id.astype(F32)
    g = jnp.where(blk_id < i, gate, -jnp.inf)
    selected = jnp.zeros(gate.shape, F32)
    for _ in range(MOBA_TOPK):
        best = jnp.max(g, axis=-1, keepdims=True)
        first = jnp.min(jnp.where(g == best, blk_f, float(LANES)), axis=-1, keepdims=True)
        pick = (blk_f == first) & (best > -jnp.inf)
        selected = jnp.where(pick, 1.0, selected)
        g = jnp.where(pick, -jnp.inf, g)
    qaug_ref[:, 0:LANES] = q2
    qaug_ref[:, LANES:2 * LANES] = jnp.where(selected > 0.0, 0.0, MASKED).astype(BF16)

    own = pl.multiple_of(i * blk, blk)
    s = _dot_nt(q2, kaug_ref[pl.ds(own, blk), 0:LANES])
    row = lax.broadcasted_iota(jnp.int32, s.shape, 0) & (blk - 1)
    col = lax.broadcasted_iota(jnp.int32, s.shape, 1)
    s = jnp.where(col <= row, s, -jnp.inf)
    m = jnp.max(s, axis=-1, keepdims=True)
    e = jnp.exp(s - m)
    m_ref[...] = m
    l_ref[...] = jnp.sum(e, axis=-1, keepdims=True)
    acc_ref[...] = _dot(e.astype(BF16), v_ref[0, pl.ds(own, blk), :])

    def past_chunk(c, carry):
        start = pl.multiple_of(c * tk, tk)
        s = _dot_nt(qaug_ref[...], kaug_ref[pl.ds(start, tk), :])
        _online_softmax_step(s, v_ref[0, pl.ds(start, tk), :], m_ref, l_ref, acc_ref)
        return carry

    lax.fori_loop(0, (i * blk + tk - 1) // tk, past_chunk, 0)
    o = acc_ref[...] / l_ref[...]
    o_ref[0] = jnp.where(lane < HEAD_DIM, o[0:blk], o[blk:2 * blk]).astype(o_ref.dtype)


def _moba_attention(qkv, tk=512):
    batch, seq, _ = qkv.shape
    n_pairs = W_MOBA // LANES
    blk = MOBA_BLOCK
    assert seq % 1024 == 0 and seq // blk <= LANES and seq // blk >= MOBA_TOPK
    return pl.pallas_call(
        functools.partial(_moba_kernel, tk=tk),
        grid=(batch, n_pairs, seq // blk),
        in_specs=[pl.BlockSpec((1, blk, LANES), lambda b, p, i: (b, i, p)),
                  pl.BlockSpec((1, seq, LANES), lambda b, p, i: (b, 0, n_pairs + p)),
                  pl.BlockSpec((1, seq, LANES), lambda b, p, i: (b, 0, 2 * n_pairs + p))],
        out_specs=pl.BlockSpec((1, blk, LANES), lambda b, p, i: (b, i, p)),
        out_shape=jax.ShapeDtypeStruct((batch, seq, W_MOBA), BF16),
        scratch_shapes=[pltpu.VMEM((seq, 2 * LANES), BF16), pltpu.VMEM((LANES, LANES), BF16),
                        pltpu.VMEM((LANES, LANES), BF16), pltpu.VMEM((2 * blk, 2 * LANES), BF16),
                        pltpu.VMEM((2 * blk, 1), F32), pltpu.VMEM((2 * blk, 1), F32),
                        pltpu.VMEM((2 * blk, LANES), F32)],
        compiler_params=_params(3),
        name="moba_attention",
    )(qkv, qkv, qkv)


def _mem_kv_kernel(mem_ref, nw_ref, w_ref, g64_ref, gain_ref, k_ref, v_ref):
    h = _rms(mem_ref[0], nw_ref[...]).astype(BF16)
    kv = _dot(h, w_ref[...])
    k = kv[:, 0:MEM_WIDTH]
    ms = _dot((k * k).astype(BF16), g64_ref[...])
    k_ref[0] = (k * lax.rsqrt(ms + EPS) * gain_ref[...]).astype(BF16)
    v_ref[0] = kv[:, MEM_WIDTH:2 * MEM_WIDTH].astype(BF16)


def _mem_kv(mem, norm_w, w_mkv, k_gain):
    batch, mem_len, _ = mem.shape
    out = jax.ShapeDtypeStruct((batch, mem_len, MEM_WIDTH), BF16)
    return pl.pallas_call(
        _mem_kv_kernel,
        grid=(batch,),
        in_specs=[pl.BlockSpec((1, mem_len, D_MODEL), lambda b: (b, 0, 0)), _resident((1, D_MODEL)),
                  _resident((D_MODEL, 2 * MEM_WIDTH)), _resident((MXU_DIM, MXU_DIM)), _resident((1, MEM_WIDTH))],
        out_specs=[pl.BlockSpec((1, mem_len, MEM_WIDTH), lambda b: (b, 0, 0))] * 2,
        out_shape=[out, out],
        compiler_params=_params(1),
        name="mem_kv",
    )(mem, norm_w.reshape(1, D_MODEL), w_mkv.astype(BF16), _group_mean_matrix(HEAD_DIM),
      jnp.tile(k_gain, N_HEADS_MEM).reshape(1, MEM_WIDTH))


def _post_kernel(x_ref, a_ref, b_ref, c_ref, wo_ref, ncross_ref, wmq_ref, g64_ref, qgain_ref, km_ref, vm_ref,
                 wmo_ref, nffn_ref, wgu_ref, wd_ref, o_ref, *, tm):
    mix = jnp.concatenate([a_ref[...], b_ref[...], c_ref[...]], axis=1)
    x = x_ref[...] + _dot(mix, wo_ref[...])

    q = _dot(_rms(x, ncross_ref[...]).astype(BF16), wmq_ref[...])
    ms = _dot((q * q).astype(BF16), g64_ref[...])
    qn = (q * lax.rsqrt(ms + EPS) * qgain_ref[...]).astype(BF16)
    head = _lane_iota(MEM_WIDTH) >> (HEAD_DIM.bit_length() - 1)
    q4 = jnp.concatenate([jnp.where(head == n, qn, jnp.zeros_like(qn)) for n in range(N_HEADS_MEM)], axis=0)
    s = _dot_nt(q4, km_ref[0])
    e = jnp.exp(s - jnp.max(s, axis=-1, keepdims=True))
    o4 = _dot(e.astype(BF16), vm_ref[0]) / jnp.sum(e, axis=-1, keepdims=True)
    o = jnp.where(head == 0, o4[0:tm], 0.0)
    for n in range(1, N_HEADS_MEM):
        o = o + jnp.where(head == n, o4[n * tm:(n + 1) * tm], 0.0)
    x = x + _dot(o.astype(BF16), wmo_ref[...])

    h = _rms(x, nffn_ref[...]).astype(BF16)
    y = jnp.zeros((tm, D_MODEL), F32)
    for c in range(D_FF // MXU_DIM):
        lo = c * MXU_DIM
        g = _dot(h, wgu_ref[:, lo:lo + MXU_DIM])
        u = _dot(h, wgu_ref[:, D_FF + lo:D_FF + lo + MXU_DIM])
        y = y + _dot((g * jax.nn.sigmoid(g) * u).astype(BF16), wd_ref[lo:lo + MXU_DIM, :])
    o_ref[...] = x + y


def _post_attention(x2d, outs, seq, w_out, norm_cross, w_mq, q_gain, k_mem, v_mem, w_mo, norm_ffn,
                    w_gate_up, w_down, tm=256):
    rows = x2d.shape[0]
    mem_len = k_mem.shape[1]
    tiles_per_seq = seq // tm
    row_spec = lambda width: pl.BlockSpec((tm, width), lambda i: (i, 0))
    mem_spec = pl.BlockSpec((1, mem_len, MEM_WIDTH), lambda i: (i // tiles_per_seq, 0, 0))
    scale = HEAD_DIM ** -0.5
    return pl.pallas_call(
        functools.partial(_post_kernel, tm=tm),
        grid=(rows // tm,),
        in_specs=[row_spec(D_MODEL), row_spec(W_DIFF), row_spec(W_DIL), row_spec(W_MOBA),
                  _resident((MIX_WIDTH, D_MODEL)), _resident((1, D_MODEL)), _resident((D_MODEL, MEM_WIDTH)),
                  _resident((MXU_DIM, MXU_DIM)), _resident((1, MEM_WIDTH)), mem_spec, mem_spec,
                  _resident((MEM_WIDTH, D_MODEL)), _resident((1, D_MODEL)), _resident((D_MODEL, 2 * D_FF)),
                  _resident((D_FF, D_MODEL))],
        out_specs=row_spec(D_MODEL),
        out_shape=jax.ShapeDtypeStruct((rows, D_MODEL), F32),
        compiler_params=_params(1),
        name="post_attention",
    )(x2d, *[o.reshape(rows, -1) for o in outs], w_out.astype(BF16), norm_cross.reshape(1, D_MODEL),
      w_mq.astype(BF16), _group_mean_matrix(HEAD_DIM), (jnp.tile(q_gain, N_HEADS_MEM) * scale).reshape(1, MEM_WIDTH),
      k_mem, v_mem, w_mo.astype(BF16), norm_ffn.reshape(1, D_MODEL), w_gate_up.astype(BF16), w_down.astype(BF16))


def kernel(x, mem, positions, norm_mix, w_in, qn_diff, kn_diff, lambda_q1, lambda_k1, lambda_q2, lambda_k2,
           subln_diff, qn_dil, kn_dil, qn_moba, kn_moba, w_out, norm_cross, norm_mem, w_mq, w_mkv, qn_mem,
           kn_mem, w_mo, norm_ffn, w_gate_up, w_down):
    batch, seq, _ = x.shape
    depth = w_in.shape[0]
    rows = batch * seq
    tables = _rope_tables(positions)
    x2d = x.reshape(rows, D_MODEL)
    for l in range(depth):
        gains = jnp.concatenate([
            jnp.tile(qn_diff[l], 2 * N_HEADS_DIFF) * DIFF_HALF ** -0.5,
            jnp.tile(qn_dil[l], N_HEADS_DIL) * HEAD_DIM ** -0.5,
            jnp.tile(qn_moba[l], N_HEADS_MOBA) * HEAD_DIM ** -0.5,
            jnp.tile(kn_diff[l], 2 * N_HEADS_DIFF), jnp.tile(kn_dil[l], N_HEADS_DIL),
            jnp.tile(kn_moba[l], N_HEADS_MOBA)]).reshape(1, 2 * MIX_WIDTH)
        qkv_a, qkv_b, qkv_c = _qkv_project(x2d, norm_mix[l], w_in[l], gains, tables)
        lam_init = 0.8 - 0.6 * math.exp(-0.3 * l)
        o_a = _diff_attention(qkv_a.reshape(batch, seq, -1),
                              (lambda_q1[l], lambda_k1[l], lambda_q2[l], lambda_k2[l]), subln_diff[l], lam_init)
        o_b = _dilated_attention(qkv_b.reshape(batch, seq, -1))
        o_c = _moba_attention(qkv_c.reshape(batch, seq, -1))
        k_mem, v_mem = _mem_kv(mem, norm_mem[l], w_mkv[l], kn_mem[l])
        x2d = _post_attention(x2d, (o_a, o_b, o_c), seq, w_out[l], norm_cross[l], w_mq[l], qn_mem[l], k_mem, v_mem,
                              w_mo[l], norm_ffn[l], w_gate_up[l], w_down[l])
    return x2d.reshape(batch, seq, D_MODEL)
```

```python
import functools
import math

import jax
import jax.numpy as jnp
from jax import lax
from jax.experimental import pallas as pl
from jax.experimental.pallas import tpu as pltpu

D_MODEL = 1024
HEAD_DIM = 64
N_HEADS_DIFF = 4
N_HEADS_DIL = 6
N_HEADS_MOBA = 6
DIFF_HALF = HEAD_DIM // 2
ROPE_THETA = 500000.0
ROPE_FRACTION = 4
DILATED_PAIRS = ((128, 1), (512, 4), (2048, 16))
DIL_STEPS = 128
MOBA_BLOCK = 256
MOBA_TOPK = 3
MEM_LEN = 256
N_HEADS_MEM = 4
MEM_WIDTH = N_HEADS_MEM * HEAD_DIM
D_FF = 2816
EPS = 1e-6

W_DIFF = N_HEADS_DIFF * HEAD_DIM
W_DIL = N_HEADS_DIL * HEAD_DIM
W_MOBA = N_HEADS_MOBA * HEAD_DIM
MIX_WIDTH = W_DIFF + W_DIL + W_MOBA

LANES = 128
MXU_DIM = 256
VMEM_LIMIT_BYTES = 56 * 2**20

MASKED = -1e30
LOG2_E = math.log2(math.e)

BF16 = jnp.bfloat16
F32 = jnp.float32
NT_DIMS = (((1,), (1,)), ((), ()))


def _dot(a, b):
    return jnp.dot(a, b, preferred_element_type=F32)


def _dot_nt(a, b):
    return lax.dot_general(a, b, NT_DIMS, preferred_element_type=F32)


def _params(n_grid):
    return pltpu.CompilerParams(dimension_semantics=("arbitrary",) * n_grid,
                                vmem_limit_bytes=VMEM_LIMIT_BYTES)


def _resident(shape):
    nd = len(shape)
    return pl.BlockSpec(shape, lambda *_: (0,) * nd, pipeline_mode=pl.Buffered(1))


def _rms(x, w):
    return x * lax.rsqrt(jnp.mean(x * x, axis=-1, keepdims=True) + EPS) * w


def _lane_iota(n=LANES):
    return lax.broadcasted_iota(jnp.int32, (1, n), 1)


def _rope_kernel(pos_ref, freq_ref, ch_ref, sh_ref, cd_ref, sd_ref):
    pos = pos_ref[...].astype(F32)
    lane = _lane_iota()
    for row, period, c_ref, s_ref in ((0, HEAD_DIM, ch_ref, sh_ref), (1, DIFF_HALF, cd_ref, sd_ref)):
        rot = period // ROPE_FRACTION
        within = lane & (period - 1)
        ang = pos * freq_ref[row:row + 1, :]
        sin = jnp.sin(ang)
        c_ref[...] = jnp.where(within < rot, jnp.cos(ang), 1.0)
        s_ref[...] = jnp.where(within < rot // 2, -sin, jnp.where(within < rot, sin, 0.0))


def _rope_tables(positions):
    rows = positions.size
    tm = 2048
    lane = jnp.arange(LANES)
    freqs = []
    for period in (HEAD_DIM, DIFF_HALF):
        half = period // ROPE_FRACTION // 2
        inv_freq = ROPE_THETA ** (-jnp.arange(half, dtype=F32) / half)
        freqs.append(inv_freq[(lane % period) % half])
    table = jax.ShapeDtypeStruct((rows, LANES), F32)
    return pl.pallas_call(
        _rope_kernel,
        grid=(rows // tm,),
        in_specs=[pl.BlockSpec((tm, 1), lambda i: (i, 0)), pl.BlockSpec((2, LANES), lambda i: (0, 0))],
        out_specs=[pl.BlockSpec((tm, LANES), lambda i: (i, 0))] * 4,
        out_shape=[table] * 4,
        compiler_params=_params(1),
        name="rope_tables",
    )(positions.reshape(rows, 1), jnp.stack(freqs))


def _group_mean_matrix(group):
    idx = jnp.arange(MXU_DIM) // group
    return jnp.where(idx[:, None] == idx[None, :], 1.0 / group, 0.0).astype(BF16)


def _rotate(t, cos, sin, half, period):
    first = (_lane_iota() & (period - 1)) < half
    partner = jnp.where(first, pltpu.roll(t, LANES - half, 1), pltpu.roll(t, half, 1))
    return t * cos + partner * sin


def _qkv_kernel(x_ref, nw_ref, w_ref, gain_ref, g32_ref, g64_ref, ch_ref, sh_ref, cd_ref, sd_ref,
                a_ref, b_ref, c_ref):
    h = _rms(x_ref[...], nw_ref[...]).astype(BF16)
    n_chunks = MIX_WIDTH // MXU_DIM
    for chunk in range(3 * n_chunks):
        section, col0 = divmod(chunk * MXU_DIM, MIX_WIDTH)
        y = _dot(h, w_ref[:, chunk * MXU_DIM:(chunk + 1) * MXU_DIM])
        if section < 2:
            is_diff = col0 < W_DIFF
            g_ref = g32_ref if is_diff else g64_ref
            ms = _dot((y * y).astype(BF16), g_ref[...])
            y = y * lax.rsqrt(ms + EPS) * gain_ref[:, chunk * MXU_DIM:(chunk + 1) * MXU_DIM]
        for piece in range(MXU_DIM // LANES):
            col = col0 + piece * LANES
            t = y[:, piece * LANES:(piece + 1) * LANES]
            if section < 2:
                if col < W_DIFF:
                    t = _rotate(t, cd_ref[...], sd_ref[...], DIFF_HALF // ROPE_FRACTION // 2, DIFF_HALF)
                else:
                    t = _rotate(t, ch_ref[...], sh_ref[...], HEAD_DIM // ROPE_FRACTION // 2, HEAD_DIM)
            if col < W_DIFF:
                dst = section * W_DIFF + col
                a_ref[:, dst:dst + LANES] = t.astype(BF16)
            elif col < W_DIFF + W_DIL:
                dst = section * W_DIL + col - W_DIFF
                b_ref[:, dst:dst + LANES] = t
            else:
                dst = section * W_MOBA + col - W_DIFF - W_DIL
                c_ref[:, dst:dst + LANES] = t.astype(BF16)


def _qkv_project(x2d, norm_w, w_in, gains, tables, tm=512):
    rows = x2d.shape[0]
    row_spec = lambda width: pl.BlockSpec((tm, width), lambda i: (i, 0))
    return pl.pallas_call(
        _qkv_kernel,
        grid=(rows // tm,),
        in_specs=[row_spec(D_MODEL), _resident((1, D_MODEL)), _resident((D_MODEL, 3 * MIX_WIDTH)),
                  _resident((1, 2 * MIX_WIDTH)), _resident((MXU_DIM, MXU_DIM)), _resident((MXU_DIM, MXU_DIM))]
                 + [row_spec(LANES)] * 4,
        out_specs=[row_spec(3 * W_DIFF), row_spec(3 * W_DIL), row_spec(3 * W_MOBA)],
        out_shape=[jax.ShapeDtypeStruct((rows, 3 * W_DIFF), BF16),
                   jax.ShapeDtypeStruct((rows, 3 * W_DIL), F32),
                   jax.ShapeDtypeStruct((rows, 3 * W_MOBA), BF16)],
        compiler_params=_params(1),
        name="qkv_project",
    )(x2d, norm_w.reshape(1, D_MODEL), w_in.astype(BF16), gains,
      _group_mean_matrix(DIFF_HALF), _group_mean_matrix(HEAD_DIM), *tables)


def _reduce_rows(x, pair_op, final_op):
    n = x.shape[0]
    while n > 8 and n % 2 == 0:
        n //= 2
        x = pair_op(x[:n], x[n:])
    return final_op(x, axis=0, keepdims=True)


def _softmax_step_t(s_t, v_t, cols, m_ref, l_ref, acc_ref):
    m_old = m_ref[:, cols]
    m_new = jnp.maximum(m_old, _reduce_rows(s_t, jnp.maximum, jnp.max))
    alpha = jnp.exp2(m_old - m_new)
    e = jnp.exp2(s_t - m_new)
    l_ref[:, cols] = alpha * l_ref[:, cols] + _reduce_rows(e, jnp.add, jnp.sum)
    acc_ref[:, cols] = alpha * acc_ref[:, cols] + _dot(v_t, e.astype(BF16))
    m_ref[:, cols] = m_new


def _transpose_values(v_ref, vt_ref, tk):
    for c in range(vt_ref.shape[0]):
        vt_ref[c] = v_ref[0, c * tk:(c + 1) * tk, :].astype(F32).T.astype(BF16)


def _diff_kernel(lq1_ref, lk1_ref, lq2_ref, lk2_ref, subln_ref, q_ref, k_ref, v_ref, o_ref,
                 vt_ref, q4t_ref, s0_ref, s1_ref, m_ref, l_ref, acc_ref, *, tq, tk, lam_init):
    i = pl.program_id(2)

    @pl.when(i == 0)
    def _():
        _transpose_values(v_ref, vt_ref, tk)

    dim = lax.broadcasted_iota(jnp.int32, (LANES, 1), 0)
    q_t = q_ref[0].astype(F32).T
    for n in range(4):
        keep = (dim >= n * DIFF_HALF) & (dim < (n + 1) * DIFF_HALF)
        q4t_ref[:, n * tq:(n + 1) * tq] = jnp.where(keep, q_t, 0.0).astype(BF16)
    m_ref[...] = jnp.full(m_ref.shape, MASKED, F32)
    l_ref[...] = jnp.zeros(l_ref.shape, F32)
    acc_ref[...] = jnp.zeros(acc_ref.shape, F32)

    def scores(j, s_ref, masked):
        k = k_ref[0, pl.ds(pl.multiple_of(j * tk, tk), tk), :]
        s_t = _dot(k, q4t_ref[...])
        if masked:
            key_pos = lax.broadcasted_iota(jnp.int32, s_t.shape, 0) + j * tk
            query_pos = (lax.broadcasted_iota(jnp.int32, s_t.shape, 1) & (tq - 1)) + i * tq
            s_t = jnp.where(key_pos <= query_pos, s_t, -jnp.inf)
        s_ref[...] = s_t

    def update(j, s_ref):
        _softmax_step_t(s_ref[...], vt_ref[j], slice(None), m_ref, l_ref, acc_ref)

    n_full = (i * tq) // tk
    scores(n_full, s0_ref, True)

    def chunk_pair(u, carry):
        scores(2 * u, s1_ref, False)
        update(jnp.where(u == 0, n_full, 2 * u - 1), s0_ref)
        scores(2 * u + 1, s0_ref, False)
        update(2 * u, s1_ref)
        return carry

    n_pairs = n_full // 2
    lax.fori_loop(0, n_pairs, chunk_pair, 0)
    pending = jnp.where(n_pairs == 0, n_full, 2 * n_pairs - 1)

    @pl.when(n_full % 2 == 1)
    def _():
        scores(n_full - 1, s1_ref, False)
        update(pending, s0_ref)
        update(n_full - 1, s1_ref)

    @pl.when(n_full % 2 == 0)
    def _():
        update(pending, s0_ref)

    lam = (jnp.exp(jnp.sum(lq1_ref[...] * lk1_ref[...], axis=-1, keepdims=True))
           - jnp.exp(jnp.sum(lq2_ref[...] * lk2_ref[...], axis=-1, keepdims=True)) + lam_init)
    o4 = acc_ref[...] / l_ref[...]
    first = dim < HEAD_DIM
    o = jnp.where(first, o4[:, 0:tq] - lam * o4[:, tq:2 * tq], o4[:, 2 * tq:3 * tq] - lam * o4[:, 3 * tq:4 * tq])
    sq = o * o
    ms0 = jnp.sum(jnp.where(first, sq, 0.0), axis=0, keepdims=True) * (1.0 / HEAD_DIM)
    ms1 = jnp.sum(jnp.where(first, 0.0, sq), axis=0, keepdims=True) * (1.0 / HEAD_DIM)
    o = o * lax.rsqrt(jnp.where(first, ms0, ms1) + EPS) * subln_ref[...] * (1.0 - lam_init)
    o_ref[0] = o.T.astype(o_ref.dtype)


def _diff_attention(qkv, lam_vecs, subln, lam_init, tq=256, tk=512):
    batch, seq, _ = qkv.shape
    n_pairs = W_DIFF // LANES
    vec = _resident((1, DIFF_HALF))
    kernel = functools.partial(_diff_kernel, tq=tq, tk=tk, lam_init=lam_init)
    return pl.pallas_call(
        kernel,
        grid=(batch, n_pairs, seq // tq),
        in_specs=[vec, vec, vec, vec, _resident((LANES, 1)),
                  pl.BlockSpec((1, tq, LANES), lambda b, p, i: (b, i, p)),
                  pl.BlockSpec((1, seq, LANES), lambda b, p, i: (b, 0, n_pairs + p)),
                  pl.BlockSpec((1, seq, LANES), lambda b, p, i: (b, 0, 2 * n_pairs + p))],
        out_specs=pl.BlockSpec((1, tq, LANES), lambda b, p, i: (b, i, p)),
        out_shape=jax.ShapeDtypeStruct((batch, seq, W_DIFF), BF16),
        scratch_shapes=[pltpu.VMEM((seq // tk, LANES, tk), BF16), pltpu.VMEM((LANES, 4 * tq), BF16),
                        pltpu.VMEM((tk, 4 * tq), F32), pltpu.VMEM((tk, 4 * tq), F32),
                        pltpu.VMEM((1, 4 * tq), F32), pltpu.VMEM((1, 4 * tq), F32),
                        pltpu.VMEM((LANES, 4 * tq), F32)],
        compiler_params=_params(3),
        name="diff_attention",
    )(*[v.reshape(1, DIFF_HALF) for v in lam_vecs], jnp.tile(subln, 2).reshape(LANES, 1), qkv, qkv, qkv)


def _dil_kernel(q_ref, k_ref, v_ref, o_ref, o0_ref, o1_ref, o2_ref, l0_ref, l1_ref, l2_ref, *, tile):
    base = pl.program_id(2) * tile
    w = DIL_STEPS
    lane = _lane_iota()
    qi = lax.broadcasted_iota(jnp.int32, (2 * w, 2 * w), 0) & (w - 1)
    kj = lax.broadcasted_iota(jnp.int32, (2 * w, 2 * w), 1)
    band = (kj >= qi) & (kj <= qi + w)

    for (window, d), ob_ref, lse_ref in zip(DILATED_PAIRS, (o0_ref, o1_ref, o2_ref), (l0_ref, l1_ref, l2_ref)):
        def rows(ref, start, d=d):
            if d == 1:
                return ref[0, pl.ds(start, w), :]
            return ref[0, pl.ds(start, w, stride=d), :]

        def block(blk, carry, window=window, d=d, ob_ref=ob_ref, lse_ref=lse_ref, rows=rows):
            local = (blk // d) * window + (blk & (d - 1))
            own = base + local
            has_prev = own >= window
            prev = jnp.where(has_prev, own - window, own)
            q = rows(q_ref, local)
            q2 = jnp.concatenate([jnp.where(lane < HEAD_DIM, q, 0.0), jnp.where(lane < HEAD_DIM, 0.0, q)], axis=0)
            k2 = jnp.concatenate([rows(k_ref, prev), rows(k_ref, own)], axis=0)
            v2 = jnp.concatenate([rows(v_ref, prev), rows(v_ref, own)], axis=0)
            s = _dot_nt(q2.astype(BF16), k2.astype(BF16))
            s = jnp.where(band & (kj >= jnp.where(has_prev, 0, w)), s, -jnp.inf)
            m = jnp.max(s, axis=-1, keepdims=True)
            e = jnp.exp(s - m)
            l = jnp.sum(e, axis=-1, keepdims=True)
            o = _dot(e.astype(BF16), v2.astype(BF16)) / l
            lse = m + jnp.log(l)
            ob = jnp.where(lane < HEAD_DIM, o[0:w], o[w:2 * w])
            lb = jnp.where(lane < HEAD_DIM, lse[0:w], lse[w:2 * w])
            if d == 1:
                ob_ref[pl.ds(local, w), :] = ob
                lse_ref[pl.ds(local, w), :] = lb
            else:
                ob_ref[pl.ds(local, w, stride=d), :] = ob
                lse_ref[pl.ds(local, w, stride=d), :] = lb
            return carry

        lax.fori_loop(0, tile // w, block, 0)

    lse = [l0_ref[...], l1_ref[...], l2_ref[...]]
    m = jnp.maximum(jnp.maximum(lse[0], lse[1]), lse[2])
    wts = [jnp.exp(x - m) for x in lse]
    mixed = wts[0] * o0_ref[...] + wts[1] * o1_ref[...] + wts[2] * o2_ref[...]
    o_ref[0] = (mixed / (wts[0] + wts[1] + wts[2])).astype(o_ref.dtype)


def _dilated_attention(qkv):
    batch, seq, _ = qkv.shape
    n_pairs = W_DIL // LANES
    tile = max(window for window, _ in DILATED_PAIRS)
    assert seq % tile == 0
    return pl.pallas_call(
        functools.partial(_dil_kernel, tile=tile),
        grid=(batch, n_pairs, seq // tile),
        in_specs=[pl.BlockSpec((1, tile, LANES), lambda b, p, i: (b, i, p)),
                  pl.BlockSpec((1, seq, LANES), lambda b, p, i: (b, 0, n_pairs + p)),
                  pl.BlockSpec((1, seq, LANES), lambda b, p, i: (b, 0, 2 * n_pairs + p))],
        out_specs=pl.BlockSpec((1, tile, LANES), lambda b, p, i: (b, i, p)),
        out_shape=jax.ShapeDtypeStruct((batch, seq, W_DIL), BF16),
        scratch_shapes=[pltpu.VMEM((tile, LANES), F32)] * 6,
        compiler_params=_params(3),
        name="dilated_attention",
    )(qkv, qkv, qkv)


def _moba_kernel(q_ref, k_ref, v_ref, o_ref, kaug_ref, vt_ref, kmh_ref, kml_ref, qaugt_ref, so_ref, s0_ref, s1_ref,
                 m_ref, l_ref, acc_ref):
    i = pl.program_id(2)
    blk = MOBA_BLOCK
    tk = 2 * blk
    blk_shift = blk.bit_length() - 1
    seq = k_ref.shape[1]

    @pl.when(i == 0)
    def _():
        _transpose_values(v_ref, vt_ref, blk)
        piece = 1024
        total = jnp.zeros((LANES, LANES), F32)
        for c in range(seq // piece):
            kc = k_ref[0, c * piece:(c + 1) * piece, :]
            key_blk = (lax.broadcasted_iota(jnp.int32, (piece, LANES), 0) + c * piece) >> blk_shift
            kaug_ref[c * piece:(c + 1) * piece, 0:LANES] = kc
            kaug_ref[c * piece:(c + 1) * piece, LANES:2 * LANES] = (
                lax.broadcasted_iota(jnp.int32, (piece, LANES), 1) == key_blk).astype(BF16)
            member = (lax.broadcasted_iota(jnp.int32, (LANES, piece), 0)
                      == (lax.broadcasted_iota(jnp.int32, (LANES, piece), 1) + c * piece) >> blk_shift)
            total = total + _dot(member.astype(BF16), kc)
        k_mean = total * (1.0 / blk)
        hi = k_mean.astype(BF16)
        kmh_ref[...] = hi
        kml_ref[...] = (k_mean - hi.astype(F32)).astype(BF16)

    dim = lax.broadcasted_iota(jnp.int32, (LANES, 1), 0)
    q_t = q_ref[0].astype(F32).T
    q2t = jnp.concatenate([jnp.where(dim < HEAD_DIM, q_t, 0.0), jnp.where(dim < HEAD_DIM, 0.0, q_t)],
                          axis=1).astype(BF16)

    gate = _dot(kmh_ref[...], q2t) + _dot(kml_ref[...], q2t)
    blk_id = lax.broadcasted_iota(jnp.int32, gate.shape, 0)
    blk_f = blk_id.astype(F32)
    g = jnp.where(blk_id < i, gate, -jnp.inf)
    selected = jnp.zeros(gate.shape, F32)
    for _ in range(MOBA_TOPK):
        best = jnp.max(g, axis=0, keepdims=True)
        first = jnp.min(jnp.where(g == best, blk_f, float(LANES)), axis=0, keepdims=True)
        pick = (blk_f == first) & (best > -jnp.inf)
        selected = jnp.where(pick, 1.0, selected)
        g = jnp.where(pick, -jnp.inf, g)
    qaugt_ref[0:LANES, :] = q2t
    qaugt_ref[LANES:2 * LANES, :] = jnp.where(selected > 0.0, 0.0, MASKED).astype(BF16)

    m_ref[...] = jnp.full(m_ref.shape, MASKED, F32)
    l_ref[...] = jnp.zeros(l_ref.shape, F32)
    acc_ref[...] = jnp.zeros(acc_ref.shape, F32)

    def scores(c, s_ref):
        s_ref[...] = _dot(kaug_ref[pl.ds(pl.multiple_of(c * tk, tk), tk), :], qaugt_ref[...])

    def update(c, s_ref):
        v_t = jnp.concatenate([vt_ref[2 * c], vt_ref[2 * c + 1]], axis=1)
        _softmax_step_t(s_ref[...], v_t, slice(None), m_ref, l_ref, acc_ref)

    s_own = _dot(kaug_ref[pl.ds(pl.multiple_of(i * blk, blk), blk), 0:LANES], q2t)
    key = lax.broadcasted_iota(jnp.int32, s_own.shape, 0)
    query = lax.broadcasted_iota(jnp.int32, s_own.shape, 1) & (blk - 1)
    so_ref[...] = jnp.where(key <= query, s_own, -jnp.inf)
    scores(0, s0_ref)
    _softmax_step_t(so_ref[...], vt_ref[i], slice(None), m_ref, l_ref, acc_ref)

    n = jnp.maximum((i + 1) // 2, 1)

    def chunk_pair(u, carry):
        scores(2 * u + 1, s1_ref)
        update(2 * u, s0_ref)
        scores(2 * u + 2, s0_ref)
        update(2 * u + 1, s1_ref)
        return carry

    lax.fori_loop(0, (n - 1) // 2, chunk_pair, 0)

    @pl.when(n % 2 == 1)
    def _():
        update(n - 1, s0_ref)

    @pl.when(n % 2 == 0)
    def _():
        scores(n - 1, s1_ref)
        update(n - 2, s0_ref)
        update(n - 1, s1_ref)

    o = acc_ref[...] / l_ref[...]
    o_ref[0] = jnp.where(dim < HEAD_DIM, o[:, 0:blk], o[:, blk:2 * blk]).T.astype(o_ref.dtype)


def _moba_attention(qkv):
    batch, seq, _ = qkv.shape
    n_pairs = W_MOBA // LANES
    blk = MOBA_BLOCK
    assert seq % 1024 == 0 and seq // blk <= LANES and seq // blk >= MOBA_TOPK
    return pl.pallas_call(
        _moba_kernel,
        grid=(batch, n_pairs, seq // blk),
        in_specs=[pl.BlockSpec((1, blk, LANES), lambda b, p, i: (b, i, p)),
                  pl.BlockSpec((1, seq, LANES), lambda b, p, i: (b, 0, n_pairs + p)),
                  pl.BlockSpec((1, seq, LANES), lambda b, p, i: (b, 0, 2 * n_pairs + p))],
        out_specs=pl.BlockSpec((1, blk, LANES), lambda b, p, i: (b, i, p)),
        out_shape=jax.ShapeDtypeStruct((batch, seq, W_MOBA), BF16),
        scratch_shapes=[pltpu.VMEM((seq, 2 * LANES), BF16), pltpu.VMEM((seq // blk, LANES, blk), BF16),
                        pltpu.VMEM((LANES, LANES), BF16), pltpu.VMEM((LANES, LANES), BF16),
                        pltpu.VMEM((2 * LANES, 2 * blk), BF16), pltpu.VMEM((blk, 2 * blk), F32),
                        pltpu.VMEM((2 * blk, 2 * blk), F32), pltpu.VMEM((2 * blk, 2 * blk), F32),
                        pltpu.VMEM((1, 2 * blk), F32), pltpu.VMEM((1, 2 * blk), F32),
                        pltpu.VMEM((LANES, 2 * blk), F32)],
        compiler_params=_params(3),
        name="moba_attention",
    )(qkv, qkv, qkv)


def _mem_kv_kernel(mem_ref, nw_ref, w_ref, g64_ref, gain_ref, k_ref, v_ref):
    h = _rms(mem_ref[0], nw_ref[...]).astype(BF16)
    kv = _dot(h, w_ref[...])
    k = kv[:, 0:MEM_WIDTH]
    ms = _dot((k * k).astype(BF16), g64_ref[...])
    k_ref[0] = (k * lax.rsqrt(ms + EPS) * gain_ref[...]).astype(BF16)
    v_ref[0] = kv[:, MEM_WIDTH:2 * MEM_WIDTH].astype(BF16)


def _mem_kv(mem, norm_w, w_mkv, k_gain):
    batch, mem_len, _ = mem.shape
    out = jax.ShapeDtypeStruct((batch, mem_len, MEM_WIDTH), BF16)
    return pl.pallas_call(
        _mem_kv_kernel,
        grid=(batch,),
        in_specs=[pl.BlockSpec((1, mem_len, D_MODEL), lambda b: (b, 0, 0)), _resident((1, D_MODEL)),
                  _resident((D_MODEL, 2 * MEM_WIDTH)), _resident((MXU_DIM, MXU_DIM)), _resident((1, MEM_WIDTH))],
        out_specs=[pl.BlockSpec((1, mem_len, MEM_WIDTH), lambda b: (b, 0, 0))] * 2,
        out_shape=[out, out],
        compiler_params=_params(1),
        name="mem_kv",
    )(mem, norm_w.reshape(1, D_MODEL), w_mkv.astype(BF16), _group_mean_matrix(HEAD_DIM),
      jnp.tile(k_gain, N_HEADS_MEM).reshape(1, MEM_WIDTH))


def _post_kernel(x_ref, a_ref, b_ref, c_ref, wo_ref, ncross_ref, wmq_ref, g64_ref, qgain_ref, km_ref, vm_ref,
                 wmo_ref, nffn_ref, wgu_ref, wd_ref, o_ref, *, tm):
    mix = jnp.concatenate([a_ref[...], b_ref[...], c_ref[...]], axis=1)
    x = x_ref[...] + _dot(mix, wo_ref[...])

    q = _dot(_rms(x, ncross_ref[...]).astype(BF16), wmq_ref[...])
    ms = _dot((q * q).astype(BF16), g64_ref[...])
    qn = (q * lax.rsqrt(ms + EPS) * qgain_ref[...]).astype(BF16)
    head = _lane_iota(MEM_WIDTH) >> (HEAD_DIM.bit_length() - 1)
    q4 = jnp.concatenate([jnp.where(head == n, qn, jnp.zeros_like(qn)) for n in range(N_HEADS_MEM)], axis=0)
    s = _dot_nt(q4, km_ref[0])
    e = jnp.exp(s - jnp.max(s, axis=-1, keepdims=True))
    o4 = _dot(e.astype(BF16), vm_ref[0]) / jnp.sum(e, axis=-1, keepdims=True)
    o = jnp.where(head == 0, o4[0:tm], 0.0)
    for n in range(1, N_HEADS_MEM):
        o = o + jnp.where(head == n, o4[n * tm:(n + 1) * tm], 0.0)
    x = x + _dot(o.astype(BF16), wmo_ref[...])

    h = _rms(x, nffn_ref[...]).astype(BF16)
    y = jnp.zeros((tm, D_MODEL), F32)
    for c in range(D_FF // MXU_DIM):
        lo = c * MXU_DIM
        g = _dot(h, wgu_ref[:, lo:lo + MXU_DIM])
        u = _dot(h, wgu_ref[:, D_FF + lo:D_FF + lo + MXU_DIM])
        y = y + _dot((g * jax.nn.sigmoid(g) * u).astype(BF16), wd_ref[lo:lo + MXU_DIM, :])
    o_ref[...] = x + y


def _post_attention(x2d, outs, seq, w_out, norm_cross, w_mq, q_gain, k_mem, v_mem, w_mo, norm_ffn,
                    w_gate_up, w_down, tm=256):
    rows = x2d.shape[0]
    mem_len = k_mem.shape[1]
    tiles_per_seq = seq // tm
    row_spec = lambda width: pl.BlockSpec((tm, width), lambda i: (i, 0))
    mem_spec = pl.BlockSpec((1, mem_len, MEM_WIDTH), lambda i: (i // tiles_per_seq, 0, 0))
    scale = HEAD_DIM ** -0.5
    return pl.pallas_call(
        functools.partial(_post_kernel, tm=tm),
        grid=(rows // tm,),
        in_specs=[row_spec(D_MODEL), row_spec(W_DIFF), row_spec(W_DIL), row_spec(W_MOBA),
                  _resident((MIX_WIDTH, D_MODEL)), _resident((1, D_MODEL)), _resident((D_MODEL, MEM_WIDTH)),
                  _resident((MXU_DIM, MXU_DIM)), _resident((1, MEM_WIDTH)), mem_spec, mem_spec,
                  _resident((MEM_WIDTH, D_MODEL)), _resident((1, D_MODEL)), _resident((D_MODEL, 2 * D_FF)),
                  _resident((D_FF, D_MODEL))],
        out_specs=row_spec(D_MODEL),
        out_shape=jax.ShapeDtypeStruct((rows, D_MODEL), F32),
        compiler_params=_params(1),
        name="post_attention",
    )(x2d, *[o.reshape(rows, -1) for o in outs], w_out.astype(BF16), norm_cross.reshape(1, D_MODEL),
      w_mq.astype(BF16), _group_mean_matrix(HEAD_DIM), (jnp.tile(q_gain, N_HEADS_MEM) * scale).reshape(1, MEM_WIDTH),
      k_mem, v_mem, w_mo.astype(BF16), norm_ffn.reshape(1, D_MODEL), w_gate_up.astype(BF16), w_down.astype(BF16))


def kernel(x, mem, positions, norm_mix, w_in, qn_diff, kn_diff, lambda_q1, lambda_k1, lambda_q2, lambda_k2,
           subln_diff, qn_dil, kn_dil, qn_moba, kn_moba, w_out, norm_cross, norm_mem, w_mq, w_mkv, qn_mem,
           kn_mem, w_mo, norm_ffn, w_gate_up, w_down):
    batch, seq, _ = x.shape
    depth = w_in.shape[0]
    rows = batch * seq
    tables = _rope_tables(positions)
    x2d = x.reshape(rows, D_MODEL)
    for l in range(depth):
        gains = jnp.concatenate([
            jnp.tile(qn_diff[l], 2 * N_HEADS_DIFF) * (DIFF_HALF ** -0.5 * LOG2_E),
            jnp.tile(qn_dil[l], N_HEADS_DIL) * HEAD_DIM ** -0.5,
            jnp.tile(qn_moba[l], N_HEADS_MOBA) * (HEAD_DIM ** -0.5 * LOG2_E),
            jnp.tile(kn_diff[l], 2 * N_HEADS_DIFF), jnp.tile(kn_dil[l], N_HEADS_DIL),
            jnp.tile(kn_moba[l], N_HEADS_MOBA)]).reshape(1, 2 * MIX_WIDTH)
        qkv_a, qkv_b, qkv_c = _qkv_project(x2d, norm_mix[l], w_in[l], gains, tables)
        lam_init = 0.8 - 0.6 * math.exp(-0.3 * l)
        o_a = _diff_attention(qkv_a.reshape(batch, seq, -1),
                              (lambda_q1[l], lambda_k1[l], lambda_q2[l], lambda_k2[l]), subln_diff[l], lam_init)
        o_b = _dilated_attention(qkv_b.reshape(batch, seq, -1))
        o_c = _moba_attention(qkv_c.reshape(batch, seq, -1))
        k_mem, v_mem = _mem_kv(mem, norm_mem[l], w_mkv[l], kn_mem[l])
        x2d = _post_attention(x2d, (o_a, o_b, o_c), seq, w_out[l], norm_cross[l], w_mq[l], qn_mem[l], k_mem, v_mem,
                              w_mo[l], norm_ffn[l], w_gate_up[l], w_down[l])
    return x2d.reshape(batch, seq, D_MODEL)
```

```python
import functools
import math

import jax
import jax.numpy as jnp
from jax import lax
from jax.experimental import pallas as pl
from jax.experimental.pallas import tpu as pltpu

D_MODEL = 1024
HEAD_DIM = 64
N_HEADS_DIFF = 4
N_HEADS_DIL = 6
N_HEADS_MOBA = 6
DIFF_HALF = HEAD_DIM // 2
ROPE_THETA = 500000.0
ROPE_FRACTION = 4
DILATED_PAIRS = ((128, 1), (512, 4), (2048, 16))
DIL_STEPS = 128
MOBA_BLOCK = 256
MOBA_TOPK = 3
MEM_LEN = 256
N_HEADS_MEM = 4
MEM_WIDTH = N_HEADS_MEM * HEAD_DIM
D_FF = 2816
EPS = 1e-6

W_DIFF = N_HEADS_DIFF * HEAD_DIM
W_DIL = N_HEADS_DIL * HEAD_DIM
W_MOBA = N_HEADS_MOBA * HEAD_DIM
MIX_WIDTH = W_DIFF + W_DIL + W_MOBA

LANES = 128
MXU_DIM = 256
VMEM_LIMIT_BYTES = 56 * 2**20

MASKED = -1e30
LOG2_E = math.log2(math.e)

BF16 = jnp.bfloat16
F32 = jnp.float32
NT_DIMS = (((1,), (1,)), ((), ()))


def _dot(a, b):
    return jnp.dot(a, b, preferred_element_type=F32)


def _dot_nt(a, b):
    return lax.dot_general(a, b, NT_DIMS, preferred_element_type=F32)


def _params(n_grid):
    return pltpu.CompilerParams(dimension_semantics=("arbitrary",) * n_grid,
                                vmem_limit_bytes=VMEM_LIMIT_BYTES)


def _resident(shape):
    nd = len(shape)
    return pl.BlockSpec(shape, lambda *_: (0,) * nd, pipeline_mode=pl.Buffered(1))


def _rms(x, w):
    return x * lax.rsqrt(jnp.mean(x * x, axis=-1, keepdims=True) + EPS) * w


def _lane_iota(n=LANES):
    return lax.broadcasted_iota(jnp.int32, (1, n), 1)


def _rope_kernel(pos_ref, freq_ref, ch_ref, sh_ref, cd_ref, sd_ref):
    pos = pos_ref[...].astype(F32)
    lane = _lane_iota()
    for row, period, c_ref, s_ref in ((0, HEAD_DIM, ch_ref, sh_ref), (1, DIFF_HALF, cd_ref, sd_ref)):
        rot = period // ROPE_FRACTION
        within = lane & (period - 1)
        ang = pos * freq_ref[row:row + 1, :]
        sin = jnp.sin(ang)
        c_ref[...] = jnp.where(within < rot, jnp.cos(ang), 1.0)
        s_ref[...] = jnp.where(within < rot // 2, -sin, jnp.where(within < rot, sin, 0.0))


def _rope_tables(positions):
    rows = positions.size
    tm = 2048
    lane = jnp.arange(LANES)
    freqs = []
    for period in (HEAD_DIM, DIFF_HALF):
        half = period // ROPE_FRACTION // 2
        inv_freq = ROPE_THETA ** (-jnp.arange(half, dtype=F32) / half)
        freqs.append(inv_freq[(lane % period) % half])
    table = jax.ShapeDtypeStruct((rows, LANES), F32)
    return pl.pallas_call(
        _rope_kernel,
        grid=(rows // tm,),
        in_specs=[pl.BlockSpec((tm, 1), lambda i: (i, 0)), pl.BlockSpec((2, LANES), lambda i: (0, 0))],
        out_specs=[pl.BlockSpec((tm, LANES), lambda i: (i, 0))] * 4,
        out_shape=[table] * 4,
        compiler_params=_params(1),
        name="rope_tables",
    )(positions.reshape(rows, 1), jnp.stack(freqs))


def _group_mean_matrix(group):
    idx = jnp.arange(MXU_DIM) // group
    return jnp.where(idx[:, None] == idx[None, :], 1.0 / group, 0.0).astype(BF16)


def _rotate(t, cos, sin, half, period):
    first = (_lane_iota() & (period - 1)) < half
    partner = jnp.where(first, pltpu.roll(t, LANES - half, 1), pltpu.roll(t, half, 1))
    return t * cos + partner * sin


def _qkv_kernel(x_ref, nw_ref, w_ref, gain_ref, g32_ref, g64_ref, ch_ref, sh_ref, cd_ref, sd_ref,
                a_ref, b_ref, c_ref):
    h = _rms(x_ref[...], nw_ref[...]).astype(BF16)
    wide = 2 * MXU_DIM
    for chunk in range(3 * MIX_WIDTH // MXU_DIM):
        section, col0 = divmod(chunk * MXU_DIM, MIX_WIDTH)
        if chunk % 2 == 0:
            y_wide = _dot(h, w_ref[:, chunk * MXU_DIM:chunk * MXU_DIM + wide])
        y = y_wide[:, (chunk % 2) * MXU_DIM:(chunk % 2 + 1) * MXU_DIM]
        if section < 2:
            is_diff = col0 < W_DIFF
            g_ref = g32_ref if is_diff else g64_ref
            ms = _dot((y * y).astype(BF16), g_ref[...])
            y = y * lax.rsqrt(ms + EPS) * gain_ref[:, chunk * MXU_DIM:(chunk + 1) * MXU_DIM]
        for piece in range(MXU_DIM // LANES):
            col = col0 + piece * LANES
            t = y[:, piece * LANES:(piece + 1) * LANES]
            if section < 2:
                if col < W_DIFF:
                    t = _rotate(t, cd_ref[...], sd_ref[...], DIFF_HALF // ROPE_FRACTION // 2, DIFF_HALF)
                else:
                    t = _rotate(t, ch_ref[...], sh_ref[...], HEAD_DIM // ROPE_FRACTION // 2, HEAD_DIM)
            if col < W_DIFF:
                dst = section * W_DIFF + col
                a_ref[:, dst:dst + LANES] = t.astype(BF16)
            elif col < W_DIFF + W_DIL:
                dst = section * W_DIL + col - W_DIFF
                b_ref[:, dst:dst + LANES] = t
            else:
                dst = section * W_MOBA + col - W_DIFF - W_DIL
                c_ref[:, dst:dst + LANES] = t.astype(BF16)


def _qkv_project(x2d, norm_w, w_in, gains, tables, tm=512):
    rows = x2d.shape[0]
    row_spec = lambda width: pl.BlockSpec((tm, width), lambda i: (i, 0))
    return pl.pallas_call(
        _qkv_kernel,
        grid=(rows // tm,),
        in_specs=[row_spec(D_MODEL), _resident((1, D_MODEL)), _resident((D_MODEL, 3 * MIX_WIDTH)),
                  _resident((1, 2 * MIX_WIDTH)), _resident((MXU_DIM, MXU_DIM)), _resident((MXU_DIM, MXU_DIM))]
                 + [row_spec(LANES)] * 4,
        out_specs=[row_spec(3 * W_DIFF), row_spec(3 * W_DIL), row_spec(3 * W_MOBA)],
        out_shape=[jax.ShapeDtypeStruct((rows, 3 * W_DIFF), BF16),
                   jax.ShapeDtypeStruct((rows, 3 * W_DIL), F32),
                   jax.ShapeDtypeStruct((rows, 3 * W_MOBA), BF16)],
        compiler_params=_params(1),
        name="qkv_project",
    )(x2d, norm_w.reshape(1, D_MODEL), w_in.astype(BF16), gains,
      _group_mean_matrix(DIFF_HALF), _group_mean_matrix(HEAD_DIM), *tables)


def _reduce_rows(x, pair_op, final_op):
    n = x.shape[0]
    while n > 8 and n % 2 == 0:
        n //= 2
        x = pair_op(x[:n], x[n:])
    return final_op(x, axis=0, keepdims=True)


def _softmax_step_t(s_t, v_t, cols, m_ref, l_ref, acc_ref):
    m_old = m_ref[:, cols]
    m_new = jnp.maximum(m_old, _reduce_rows(s_t, jnp.maximum, jnp.max))
    alpha = jnp.exp2(m_old - m_new)
    e = jnp.exp2(s_t - m_new)
    l_ref[:, cols] = alpha * l_ref[:, cols] + _reduce_rows(e, jnp.add, jnp.sum)
    acc_ref[:, cols] = alpha * acc_ref[:, cols] + _dot(v_t, e.astype(BF16))
    m_ref[:, cols] = m_new


def _transpose_values(v_ref, vt_ref, tk):
    for c in range(vt_ref.shape[0]):
        vt_ref[c] = v_ref[0, c * tk:(c + 1) * tk, :].astype(F32).T.astype(BF16)


def _diff_kernel(lq1_ref, lk1_ref, lq2_ref, lk2_ref, subln_ref, q_ref, k_ref, v_ref, o_ref,
                 vt_ref, q4t_ref, s0_ref, s1_ref, m_ref, l_ref, acc_ref, *, tq, tk, lam_init):
    i = pl.program_id(2)

    @pl.when(i == 0)
    def _():
        _transpose_values(v_ref, vt_ref, tk)

    dim = lax.broadcasted_iota(jnp.int32, (LANES, 1), 0)
    q_t = q_ref[0].astype(F32).T
    for n in range(4):
        keep = (dim >= n * DIFF_HALF) & (dim < (n + 1) * DIFF_HALF)
        q4t_ref[:, n * tq:(n + 1) * tq] = jnp.where(keep, q_t, 0.0).astype(BF16)
    m_ref[...] = jnp.full(m_ref.shape, MASKED, F32)
    l_ref[...] = jnp.zeros(l_ref.shape, F32)
    acc_ref[...] = jnp.zeros(acc_ref.shape, F32)

    def scores(j, s_ref, masked):
        k = k_ref[0, pl.ds(pl.multiple_of(j * tk, tk), tk), :]
        s_t = _dot(k, q4t_ref[...])
        if masked:
            key_pos = lax.broadcasted_iota(jnp.int32, s_t.shape, 0) + j * tk
            query_pos = (lax.broadcasted_iota(jnp.int32, s_t.shape, 1) & (tq - 1)) + i * tq
            s_t = jnp.where(key_pos <= query_pos, s_t, -jnp.inf)
        s_ref[...] = s_t

    def update(j, s_ref):
        _softmax_step_t(s_ref[...], vt_ref[j], slice(None), m_ref, l_ref, acc_ref)

    n_full = (i * tq) // tk
    scores(n_full, s0_ref, True)

    def chunk_pair(u, carry):
        scores(2 * u, s1_ref, False)
        update(jnp.where(u == 0, n_full, 2 * u - 1), s0_ref)
        scores(2 * u + 1, s0_ref, False)
        update(2 * u, s1_ref)
        return carry

    n_pairs = n_full // 2
    lax.fori_loop(0, n_pairs, chunk_pair, 0)
    pending = jnp.where(n_pairs == 0, n_full, 2 * n_pairs - 1)

    @pl.when(n_full % 2 == 1)
    def _():
        scores(n_full - 1, s1_ref, False)
        update(pending, s0_ref)
        update(n_full - 1, s1_ref)

    @pl.when(n_full % 2 == 0)
    def _():
        update(pending, s0_ref)

    lam = (jnp.exp(jnp.sum(lq1_ref[...] * lk1_ref[...], axis=-1, keepdims=True))
           - jnp.exp(jnp.sum(lq2_ref[...] * lk2_ref[...], axis=-1, keepdims=True)) + lam_init)
    o4 = acc_ref[...] / l_ref[...]
    first = dim < HEAD_DIM
    o = jnp.where(first, o4[:, 0:tq] - lam * o4[:, tq:2 * tq], o4[:, 2 * tq:3 * tq] - lam * o4[:, 3 * tq:4 * tq])
    sq = o * o
    ms0 = jnp.sum(jnp.where(first, sq, 0.0), axis=0, keepdims=True) * (1.0 / HEAD_DIM)
    ms1 = jnp.sum(jnp.where(first, 0.0, sq), axis=0, keepdims=True) * (1.0 / HEAD_DIM)
    o = o * lax.rsqrt(jnp.where(first, ms0, ms1) + EPS) * subln_ref[...] * (1.0 - lam_init)
    o_ref[0] = o.T.astype(o_ref.dtype)


def _diff_attention(qkv, lam_vecs, subln, lam_init, tq=256, tk=512):
    batch, seq, _ = qkv.shape
    n_pairs = W_DIFF // LANES
    vec = _resident((1, DIFF_HALF))
    kernel = functools.partial(_diff_kernel, tq=tq, tk=tk, lam_init=lam_init)
    return pl.pallas_call(
        kernel,
        grid=(batch, n_pairs, seq // tq),
        in_specs=[vec, vec, vec, vec, _resident((LANES, 1)),
                  pl.BlockSpec((1, tq, LANES), lambda b, p, i: (b, i, p)),
                  pl.BlockSpec((1, seq, LANES), lambda b, p, i: (b, 0, n_pairs + p)),
                  pl.BlockSpec((1, seq, LANES), lambda b, p, i: (b, 0, 2 * n_pairs + p))],
        out_specs=pl.BlockSpec((1, tq, LANES), lambda b, p, i: (b, i, p)),
        out_shape=jax.ShapeDtypeStruct((batch, seq, W_DIFF), BF16),
        scratch_shapes=[pltpu.VMEM((seq // tk, LANES, tk), BF16), pltpu.VMEM((LANES, 4 * tq), BF16),
                        pltpu.VMEM((tk, 4 * tq), F32), pltpu.VMEM((tk, 4 * tq), F32),
                        pltpu.VMEM((1, 4 * tq), F32), pltpu.VMEM((1, 4 * tq), F32),
                        pltpu.VMEM((LANES, 4 * tq), F32)],
        compiler_params=_params(3),
        name="diff_attention",
    )(*[v.reshape(1, DIFF_HALF) for v in lam_vecs], jnp.tile(subln, 2).reshape(LANES, 1), qkv, qkv, qkv)


def _dil_kernel(q_ref, k_ref, v_ref, o_ref, o0_ref, o1_ref, o2_ref, l0_ref, l1_ref, l2_ref, *, tile, group):
    base = pl.program_id(2) * tile
    w = DIL_STEPS
    lane = _lane_iota()
    qi = lax.broadcasted_iota(jnp.int32, (2 * w, 2 * w), 0) & (w - 1)
    kj = lax.broadcasted_iota(jnp.int32, (2 * w, 2 * w), 1)
    band = (kj >= qi) & (kj <= qi + w)

    for (window, d), ob_ref, lse_ref in zip(DILATED_PAIRS, (o0_ref, o1_ref, o2_ref), (l0_ref, l1_ref, l2_ref)):
        def rows(ref, start, d=d):
            if d == 1:
                return ref[0, pl.ds(start, w), :]
            return ref[0, pl.ds(start, w, stride=d), :]

        def block_group(grp, carry, window=window, d=d, ob_ref=ob_ref, lse_ref=lse_ref, rows=rows):
            local, prev, own, lo, s, e, m, o = [], [], [], [], [], [], [], []
            for g in range(group):
                blk = grp * group + g
                local.append((blk // d) * window + (blk & (d - 1)))
                own.append(base + local[g])
                has_prev = own[g] >= window
                prev.append(jnp.where(has_prev, own[g] - window, own[g]))
                lo.append(jnp.where(has_prev, 0, w))
                q = rows(q_ref, local[g])
                q2 = jnp.concatenate([jnp.where(lane < HEAD_DIM, q, 0.0), jnp.where(lane < HEAD_DIM, 0.0, q)], axis=0)
                k2 = jnp.concatenate([rows(k_ref, prev[g]), rows(k_ref, own[g])], axis=0)
                s.append(_dot_nt(q2.astype(BF16), k2.astype(BF16)))
            for g in range(group):
                sg = jnp.where(band & (kj >= lo[g]), s[g], -jnp.inf)
                m.append(jnp.max(sg, axis=-1, keepdims=True))
                e.append(jnp.exp2(sg - m[g]).astype(BF16))
            for g in range(group):
                v2 = jnp.concatenate([rows(v_ref, prev[g]), rows(v_ref, own[g])], axis=0).astype(BF16)
                o.append(_dot(e[g], jnp.concatenate([v2, jnp.ones_like(v2)], axis=1)))
            for g in range(group):
                l = o[g][:, LANES:2 * LANES]
                og = o[g][:, 0:LANES] / l
                lse = m[g] + jnp.log2(l)
                ob = jnp.where(lane < HEAD_DIM, og[0:w], og[w:2 * w])
                lb = jnp.where(lane < HEAD_DIM, lse[0:w], lse[w:2 * w])
                if d == 1:
                    ob_ref[pl.ds(local[g], w), :] = ob
                    lse_ref[pl.ds(local[g], w), :] = lb
                else:
                    ob_ref[pl.ds(local[g], w, stride=d), :] = ob
                    lse_ref[pl.ds(local[g], w, stride=d), :] = lb
            return carry

        lax.fori_loop(0, tile // w // group, block_group, 0)

    lse = [l0_ref[...], l1_ref[...], l2_ref[...]]
    m = jnp.maximum(jnp.maximum(lse[0], lse[1]), lse[2])
    wts = [jnp.exp2(x - m) for x in lse]
    mixed = wts[0] * o0_ref[...] + wts[1] * o1_ref[...] + wts[2] * o2_ref[...]
    o_ref[0] = (mixed / (wts[0] + wts[1] + wts[2])).astype(o_ref.dtype)


def _dilated_attention(qkv, group=4):
    batch, seq, _ = qkv.shape
    n_pairs = W_DIL // LANES
    tile = max(window for window, _ in DILATED_PAIRS)
    assert seq % tile == 0
    return pl.pallas_call(
        functools.partial(_dil_kernel, tile=tile, group=group),
        grid=(batch, n_pairs, seq // tile),
        in_specs=[pl.BlockSpec((1, tile, LANES), lambda b, p, i: (b, i, p)),
                  pl.BlockSpec((1, seq, LANES), lambda b, p, i: (b, 0, n_pairs + p)),
                  pl.BlockSpec((1, seq, LANES), lambda b, p, i: (b, 0, 2 * n_pairs + p))],
        out_specs=pl.BlockSpec((1, tile, LANES), lambda b, p, i: (b, i, p)),
        out_shape=jax.ShapeDtypeStruct((batch, seq, W_DIL), BF16),
        scratch_shapes=[pltpu.VMEM((tile, LANES), F32)] * 6,
        compiler_params=_params(3),
        name="dilated_attention",
    )(qkv, qkv, qkv)


def _moba_kernel(q_ref, k_ref, v_ref, o_ref, kaug_ref, vt_ref, kmh_ref, kml_ref, qaugt_ref, so_ref, s0_ref, s1_ref,
                 m_ref, l_ref, acc_ref):
    i = pl.program_id(2)
    blk = MOBA_BLOCK
    tk = 2 * blk
    blk_shift = blk.bit_length() - 1
    seq = k_ref.shape[1]

    @pl.when(i == 0)
    def _():
        _transpose_values(v_ref, vt_ref, blk)
        piece = 1024
        total = jnp.zeros((LANES, LANES), F32)
        for c in range(seq // piece):
            kc = k_ref[0, c * piece:(c + 1) * piece, :]
            key_blk = (lax.broadcasted_iota(jnp.int32, (piece, LANES), 0) + c * piece) >> blk_shift
            kaug_ref[c * piece:(c + 1) * piece, 0:LANES] = kc
            kaug_ref[c * piece:(c + 1) * piece, LANES:2 * LANES] = (
                lax.broadcasted_iota(jnp.int32, (piece, LANES), 1) == key_blk).astype(BF16)
            member = (lax.broadcasted_iota(jnp.int32, (LANES, piece), 0)
                      == (lax.broadcasted_iota(jnp.int32, (LANES, piece), 1) + c * piece) >> blk_shift)
            total = total + _dot(member.astype(BF16), kc)
        k_mean = total * (1.0 / blk)
        hi = k_mean.astype(BF16)
        kmh_ref[...] = hi
        kml_ref[...] = (k_mean - hi.astype(F32)).astype(BF16)

    dim = lax.broadcasted_iota(jnp.int32, (LANES, 1), 0)
    q_t = q_ref[0].astype(F32).T
    q2t = jnp.concatenate([jnp.where(dim < HEAD_DIM, q_t, 0.0), jnp.where(dim < HEAD_DIM, 0.0, q_t)],
                          axis=1).astype(BF16)

    gate = _dot(kmh_ref[...], q2t) + _dot(kml_ref[...], q2t)
    blk_id = lax.broadcasted_iota(jnp.int32, gate.shape, 0)
    blk_f = blk_id.astype(F32)
    g = jnp.where(blk_id < i, gate, -jnp.inf)
    selected = jnp.zeros(gate.shape, F32)
    for _ in range(MOBA_TOPK):
        best = jnp.max(g, axis=0, keepdims=True)
        first = jnp.min(jnp.where(g == best, blk_f, float(LANES)), axis=0, keepdims=True)
        pick = (blk_f == first) & (best > -jnp.inf)
        selected = jnp.where(pick, 1.0, selected)
        g = jnp.where(pick, -jnp.inf, g)
    qaugt_ref[0:LANES, :] = q2t
    qaugt_ref[LANES:2 * LANES, :] = jnp.where(selected > 0.0, 0.0, MASKED).astype(BF16)

    m_ref[...] = jnp.full(m_ref.shape, MASKED, F32)
    l_ref[...] = jnp.zeros(l_ref.shape, F32)
    acc_ref[...] = jnp.zeros(acc_ref.shape, F32)

    def scores(c, s_ref):
        s_ref[...] = _dot(kaug_ref[pl.ds(pl.multiple_of(c * tk, tk), tk), :], qaugt_ref[...])

    def update(c, s_ref):
        v_t = jnp.concatenate([vt_ref[2 * c], vt_ref[2 * c + 1]], axis=1)
        _softmax_step_t(s_ref[...], v_t, slice(None), m_ref, l_ref, acc_ref)

    s_own = _dot(kaug_ref[pl.ds(pl.multiple_of(i * blk, blk), blk), 0:LANES], q2t)
    key = lax.broadcasted_iota(jnp.int32, s_own.shape, 0)
    query = lax.broadcasted_iota(jnp.int32, s_own.shape, 1) & (blk - 1)
    so_ref[...] = jnp.where(key <= query, s_own, -jnp.inf)
    scores(0, s0_ref)
    _softmax_step_t(so_ref[...], vt_ref[i], slice(None), m_ref, l_ref, acc_ref)

    n = jnp.maximum((i + 1) // 2, 1)

    def chunk_pair(u, carry):
        scores(2 * u + 1, s1_ref)
        update(2 * u, s0_ref)
        scores(2 * u + 2, s0_ref)
        update(2 * u + 1, s1_ref)
        return carry

    lax.fori_loop(0, (n - 1) // 2, chunk_pair, 0)

    @pl.when(n % 2 == 1)
    def _():
        update(n - 1, s0_ref)

    @pl.when(n % 2 == 0)
    def _():
        scores(n - 1, s1_ref)
        update(n - 2, s0_ref)
        update(n - 1, s1_ref)

    o = acc_ref[...] / l_ref[...]
    o_ref[0] = jnp.where(dim < HEAD_DIM, o[:, 0:blk], o[:, blk:2 * blk]).T.astype(o_ref.dtype)


def _moba_attention(qkv):
    batch, seq, _ = qkv.shape
    n_pairs = W_MOBA // LANES
    blk = MOBA_BLOCK
    assert seq % 1024 == 0 and seq // blk <= LANES and seq // blk >= MOBA_TOPK
    return pl.pallas_call(
        _moba_kernel,
        grid=(batch, n_pairs, seq // blk),
        in_specs=[pl.BlockSpec((1, blk, LANES), lambda b, p, i: (b, i, p)),
                  pl.BlockSpec((1, seq, LANES), lambda b, p, i: (b, 0, n_pairs + p)),
                  pl.BlockSpec((1, seq, LANES), lambda b, p, i: (b, 0, 2 * n_pairs + p))],
        out_specs=pl.BlockSpec((1, blk, LANES), lambda b, p, i: (b, i, p)),
        out_shape=jax.ShapeDtypeStruct((batch, seq, W_MOBA), BF16),
        scratch_shapes=[pltpu.VMEM((seq, 2 * LANES), BF16), pltpu.VMEM((seq // blk, LANES, blk), BF16),
                        pltpu.VMEM((LANES, LANES), BF16), pltpu.VMEM((LANES, LANES), BF16),
                        pltpu.VMEM((2 * LANES, 2 * blk), BF16), pltpu.VMEM((blk, 2 * blk), F32),
                        pltpu.VMEM((2 * blk, 2 * blk), F32), pltpu.VMEM((2 * blk, 2 * blk), F32),
                        pltpu.VMEM((1, 2 * blk), F32), pltpu.VMEM((1, 2 * blk), F32),
                        pltpu.VMEM((LANES, 2 * blk), F32)],
        compiler_params=_params(3),
        name="moba_attention",
    )(qkv, qkv, qkv)


def _mem_kv_kernel(mem_ref, nw_ref, w_ref, g64_ref, gain_ref, k_ref, v_ref):
    h = _rms(mem_ref[0], nw_ref[...]).astype(BF16)
    kv = _dot(h, w_ref[...])
    k = kv[:, 0:MEM_WIDTH]
    ms = _dot((k * k).astype(BF16), g64_ref[...])
    k_ref[0] = (k * lax.rsqrt(ms + EPS) * gain_ref[...]).astype(BF16)
    v_ref[0] = kv[:, MEM_WIDTH:2 * MEM_WIDTH].astype(BF16)


def _mem_kv(mem, norm_w, w_mkv, k_gain):
    batch, mem_len, _ = mem.shape
    out = jax.ShapeDtypeStruct((batch, mem_len, MEM_WIDTH), BF16)
    return pl.pallas_call(
        _mem_kv_kernel,
        grid=(batch,),
        in_specs=[pl.BlockSpec((1, mem_len, D_MODEL), lambda b: (b, 0, 0)), _resident((1, D_MODEL)),
                  _resident((D_MODEL, 2 * MEM_WIDTH)), _resident((MXU_DIM, MXU_DIM)), _resident((1, MEM_WIDTH))],
        out_specs=[pl.BlockSpec((1, mem_len, MEM_WIDTH), lambda b: (b, 0, 0))] * 2,
        out_shape=[out, out],
        compiler_params=_params(1),
        name="mem_kv",
    )(mem, norm_w.reshape(1, D_MODEL), w_mkv.astype(BF16), _group_mean_matrix(HEAD_DIM),
      jnp.tile(k_gain, N_HEADS_MEM).reshape(1, MEM_WIDTH))


def _post_kernel(x_ref, a_ref, b_ref, c_ref, wo_ref, ncross_ref, wmq_ref, g64_ref, qgain_ref, km_ref, vm_ref,
                 wmo_ref, nffn_ref, wgu_ref, wd_ref, o_ref, *, tm):
    mix = jnp.concatenate([a_ref[...], b_ref[...], c_ref[...]], axis=1)
    x = x_ref[...] + _dot(mix, wo_ref[...])

    q = _dot(_rms(x, ncross_ref[...]).astype(BF16), wmq_ref[...])
    ms = _dot((q * q).astype(BF16), g64_ref[...])
    qn = (q * lax.rsqrt(ms + EPS) * qgain_ref[...]).astype(BF16)
    head = _lane_iota(MEM_WIDTH) >> (HEAD_DIM.bit_length() - 1)
    q4 = jnp.concatenate([jnp.where(head == n, qn, jnp.zeros_like(qn)) for n in range(N_HEADS_MEM)], axis=0)
    s = _dot_nt(q4, km_ref[0])
    e = jnp.exp(s - jnp.max(s, axis=-1, keepdims=True))
    o4 = _dot(e.astype(BF16), vm_ref[0]) / jnp.sum(e, axis=-1, keepdims=True)
    o = jnp.where(head == 0, o4[0:tm], 0.0)
    for n in range(1, N_HEADS_MEM):
        o = o + jnp.where(head == n, o4[n * tm:(n + 1) * tm], 0.0)
    x = x + _dot(o.astype(BF16), wmo_ref[...])

    h = _rms(x, nffn_ref[...]).astype(BF16)
    y = jnp.zeros((tm, D_MODEL), F32)
    for c in range(D_FF // MXU_DIM):
        lo = c * MXU_DIM
        g = _dot(h, wgu_ref[:, lo:lo + MXU_DIM])
        u = _dot(h, wgu_ref[:, D_FF + lo:D_FF + lo + MXU_DIM])
        y = y + _dot((g * jax.nn.sigmoid(g) * u).astype(BF16), wd_ref[lo:lo + MXU_DIM, :])
    o_ref[...] = x + y


def _post_attention(x2d, outs, seq, w_out, norm_cross, w_mq, q_gain, k_mem, v_mem, w_mo, norm_ffn,
                    w_gate_up, w_down, tm=512):
    rows = x2d.shape[0]
    mem_len = k_mem.shape[1]
    tiles_per_seq = seq // tm
    row_spec = lambda width: pl.BlockSpec((tm, width), lambda i: (i, 0))
    mem_spec = pl.BlockSpec((1, mem_len, MEM_WIDTH), lambda i: (i // tiles_per_seq, 0, 0))
    scale = HEAD_DIM ** -0.5
    return pl.pallas_call(
        functools.partial(_post_kernel, tm=tm),
        grid=(rows // tm,),
        in_specs=[row_spec(D_MODEL), row_spec(W_DIFF), row_spec(W_DIL), row_spec(W_MOBA),
                  _resident((MIX_WIDTH, D_MODEL)), _resident((1, D_MODEL)), _resident((D_MODEL, MEM_WIDTH)),
                  _resident((MXU_DIM, MXU_DIM)), _resident((1, MEM_WIDTH)), mem_spec, mem_spec,
                  _resident((MEM_WIDTH, D_MODEL)), _resident((1, D_MODEL)), _resident((D_MODEL, 2 * D_FF)),
                  _resident((D_FF, D_MODEL))],
        out_specs=row_spec(D_MODEL),
        out_shape=jax.ShapeDtypeStruct((rows, D_MODEL), F32),
        compiler_params=_params(1),
        name="post_attention",
    )(x2d, *[o.reshape(rows, -1) for o in outs], w_out.astype(BF16), norm_cross.reshape(1, D_MODEL),
      w_mq.astype(BF16), _group_mean_matrix(HEAD_DIM), (jnp.tile(q_gain, N_HEADS_MEM) * scale).reshape(1, MEM_WIDTH),
      k_mem, v_mem, w_mo.astype(BF16), norm_ffn.reshape(1, D_MODEL), w_gate_up.astype(BF16), w_down.astype(BF16))


def kernel(x, mem, positions, norm_mix, w_in, qn_diff, kn_diff, lambda_q1, lambda_k1, lambda_q2, lambda_k2,
           subln_diff, qn_dil, kn_dil, qn_moba, kn_moba, w_out, norm_cross, norm_mem, w_mq, w_mkv, qn_mem,
           kn_mem, w_mo, norm_ffn, w_gate_up, w_down):
    batch, seq, _ = x.shape
    depth = w_in.shape[0]
    rows = batch * seq
    tables = _rope_tables(positions)
    x2d = x.reshape(rows, D_MODEL)
    for l in range(depth):
        gains = jnp.concatenate([
            jnp.tile(qn_diff[l], 2 * N_HEADS_DIFF) * (DIFF_HALF ** -0.5 * LOG2_E),
            jnp.tile(qn_dil[l], N_HEADS_DIL) * (HEAD_DIM ** -0.5 * LOG2_E),
            jnp.tile(qn_moba[l], N_HEADS_MOBA) * (HEAD_DIM ** -0.5 * LOG2_E),
            jnp.tile(kn_diff[l], 2 * N_HEADS_DIFF), jnp.tile(kn_dil[l], N_HEADS_DIL),
            jnp.tile(kn_moba[l], N_HEADS_MOBA)]).reshape(1, 2 * MIX_WIDTH)
        qkv_a, qkv_b, qkv_c = _qkv_project(x2d, norm_mix[l], w_in[l], gains, tables)
        lam_init = 0.8 - 0.6 * math.exp(-0.3 * l)
        o_a = _diff_attention(qkv_a.reshape(batch, seq, -1),
                              (lambda_q1[l], lambda_k1[l], lambda_q2[l], lambda_k2[l]), subln_diff[l], lam_init)
        o_b = _dilated_attention(qkv_b.reshape(batch, seq, -1))
        o_c = _moba_attention(qkv_c.reshape(batch, seq, -1))
        k_mem, v_mem = _mem_kv(mem, norm_mem[l], w_mkv[l], kn_mem[l])
        x2d = _post_attention(x2d, (o_a, o_b, o_c), seq, w_out[l], norm_cross[l], w_mq[l], qn_mem[l], k_mem, v_mem,
                              w_mo[l], norm_ffn[l], w_gate_up[l], w_down[l])
    return x2d.reshape(batch, seq, D_MODEL)
```

```python
import functools
import math

import jax
import jax.numpy as jnp
from jax import lax
from jax.experimental import pallas as pl
from jax.experimental.pallas import tpu as pltpu

D_MODEL = 1024
HEAD_DIM = 64
N_HEADS_DIFF = 4
N_HEADS_DIL = 6
N_HEADS_MOBA = 6
DIFF_HALF = HEAD_DIM // 2
ROPE_THETA = 500000.0
ROPE_FRACTION = 4
DILATED_PAIRS = ((128, 1), (512, 4), (2048, 16))
DIL_STEPS = 128
MOBA_BLOCK = 256
MOBA_TOPK = 3
MEM_LEN = 256
N_HEADS_MEM = 4
MEM_WIDTH = N_HEADS_MEM * HEAD_DIM
D_FF = 2816
EPS = 1e-6

W_DIFF = N_HEADS_DIFF * HEAD_DIM
W_DIL = N_HEADS_DIL * HEAD_DIM
W_MOBA = N_HEADS_MOBA * HEAD_DIM
MIX_WIDTH = W_DIFF + W_DIL + W_MOBA

LANES = 128
MXU_DIM = 256
VMEM_LIMIT_BYTES = 56 * 2**20

MASKED = -1e30
LOG2_E = math.log2(math.e)

BF16 = jnp.bfloat16
F32 = jnp.float32
NT_DIMS = (((1,), (1,)), ((), ()))


def _dot(a, b):
    return jnp.dot(a, b, preferred_element_type=F32)


def _dot_nt(a, b):
    return lax.dot_general(a, b, NT_DIMS, preferred_element_type=F32)


def _params(n_grid):
    return pltpu.CompilerParams(dimension_semantics=("arbitrary",) * n_grid,
                                vmem_limit_bytes=VMEM_LIMIT_BYTES)


def _resident(shape):
    nd = len(shape)
    return pl.BlockSpec(shape, lambda *_: (0,) * nd, pipeline_mode=pl.Buffered(1))


def _rms(x, w):
    return x * lax.rsqrt(jnp.mean(x * x, axis=-1, keepdims=True) + EPS) * w


def _lane_iota(n=LANES):
    return lax.broadcasted_iota(jnp.int32, (1, n), 1)


def _rope_kernel(pos_ref, freq_ref, ch_ref, sh_ref, cd_ref, sd_ref):
    pos = pos_ref[...].astype(F32)
    lane = _lane_iota()
    for row, period, c_ref, s_ref in ((0, HEAD_DIM, ch_ref, sh_ref), (1, DIFF_HALF, cd_ref, sd_ref)):
        rot = period // ROPE_FRACTION
        within = lane & (period - 1)
        ang = pos * freq_ref[row:row + 1, :]
        sin = jnp.sin(ang)
        c_ref[...] = jnp.where(within < rot, jnp.cos(ang), 1.0)
        s_ref[...] = jnp.where(within < rot // 2, -sin, jnp.where(within < rot, sin, 0.0))


def _rope_tables(positions):
    rows = positions.size
    tm = 2048
    lane = jnp.arange(LANES)
    freqs = []
    for period in (HEAD_DIM, DIFF_HALF):
        half = period // ROPE_FRACTION // 2
        inv_freq = ROPE_THETA ** (-jnp.arange(half, dtype=F32) / half)
        freqs.append(inv_freq[(lane % period) % half])
    table = jax.ShapeDtypeStruct((rows, LANES), F32)
    return pl.pallas_call(
        _rope_kernel,
        grid=(rows // tm,),
        in_specs=[pl.BlockSpec((tm, 1), lambda i: (i, 0)), pl.BlockSpec((2, LANES), lambda i: (0, 0))],
        out_specs=[pl.BlockSpec((tm, LANES), lambda i: (i, 0))] * 4,
        out_shape=[table] * 4,
        compiler_params=_params(1),
        name="rope_tables",
    )(positions.reshape(rows, 1), jnp.stack(freqs))


def _group_mean_matrix(group):
    idx = jnp.arange(MXU_DIM) // group
    return jnp.where(idx[:, None] == idx[None, :], 1.0 / group, 0.0).astype(BF16)


def _rotate(t, cos, sin, half, period):
    first = (_lane_iota() & (period - 1)) < half
    partner = jnp.where(first, pltpu.roll(t, LANES - half, 1), pltpu.roll(t, half, 1))
    return t * cos + partner * sin


def _qkv_kernel(x_ref, nw_ref, w_ref, gain_ref, g32_ref, g64_ref, ch_ref, sh_ref, cd_ref, sd_ref,
                a_ref, b_ref, c_ref):
    h = _rms(x_ref[...], nw_ref[...]).astype(BF16)
    wide = 2 * MXU_DIM
    for chunk in range(3 * MIX_WIDTH // MXU_DIM):
        section, col0 = divmod(chunk * MXU_DIM, MIX_WIDTH)
        if chunk % 2 == 0:
            y_wide = _dot(h, w_ref[:, chunk * MXU_DIM:chunk * MXU_DIM + wide])
        y = y_wide[:, (chunk % 2) * MXU_DIM:(chunk % 2 + 1) * MXU_DIM]
        if section < 2:
            is_diff = col0 < W_DIFF
            g_ref = g32_ref if is_diff else g64_ref
            ms = _dot((y * y).astype(BF16), g_ref[...])
            y = y * lax.rsqrt(ms + EPS) * gain_ref[:, chunk * MXU_DIM:(chunk + 1) * MXU_DIM]
        for piece in range(MXU_DIM // LANES):
            col = col0 + piece * LANES
            t = y[:, piece * LANES:(piece + 1) * LANES]
            if section < 2:
                if col < W_DIFF:
                    t = _rotate(t, cd_ref[...], sd_ref[...], DIFF_HALF // ROPE_FRACTION // 2, DIFF_HALF)
                else:
                    t = _rotate(t, ch_ref[...], sh_ref[...], HEAD_DIM // ROPE_FRACTION // 2, HEAD_DIM)
            if col < W_DIFF:
                dst = section * W_DIFF + col
                a_ref[:, dst:dst + LANES] = t.astype(BF16)
            elif col < W_DIFF + W_DIL:
                dst = section * W_DIL + col - W_DIFF
                b_ref[:, dst:dst + LANES] = t
            else:
                dst = section * W_MOBA + col - W_DIFF - W_DIL
                c_ref[:, dst:dst + LANES] = t.astype(BF16)


def _qkv_project(x2d, norm_w, w_in, gains, tables, tm=512):
    rows = x2d.shape[0]
    row_spec = lambda width: pl.BlockSpec((tm, width), lambda i: (i, 0))
    return pl.pallas_call(
        _qkv_kernel,
        grid=(rows // tm,),
        in_specs=[row_spec(D_MODEL), _resident((1, D_MODEL)), _resident((D_MODEL, 3 * MIX_WIDTH)),
                  _resident((1, 2 * MIX_WIDTH)), _resident((MXU_DIM, MXU_DIM)), _resident((MXU_DIM, MXU_DIM))]
                 + [row_spec(LANES)] * 4,
        out_specs=[row_spec(3 * W_DIFF), row_spec(3 * W_DIL), row_spec(3 * W_MOBA)],
        out_shape=[jax.ShapeDtypeStruct((rows, 3 * W_DIFF), BF16),
                   jax.ShapeDtypeStruct((rows, 3 * W_DIL), F32),
                   jax.ShapeDtypeStruct((rows, 3 * W_MOBA), BF16)],
        compiler_params=_params(1),
        name="qkv_project",
    )(x2d, norm_w.reshape(1, D_MODEL), w_in.astype(BF16), gains,
      _group_mean_matrix(DIFF_HALF), _group_mean_matrix(HEAD_DIM), *tables)


def _reduce_rows(x, pair_op, final_op):
    n = x.shape[0]
    while n > 8 and n % 2 == 0:
        n //= 2
        x = pair_op(x[:n], x[n:])
    return final_op(x, axis=0, keepdims=True)


ONES_ROWS = 16


def _store_scores(s_t, buf, cols):
    s_ref, smax_ref = buf
    s_ref[:, cols] = s_t
    smax_ref[:, cols] = _reduce_rows(s_t, jnp.maximum, jnp.max)


def _softmax_update_t(buf, v_ts, m_ref, l_ref, acc_ref):
    s_ref, smax_ref = buf
    width = s_ref.shape[1] // len(v_ts)
    m_old = m_ref[...]
    m_new = jnp.maximum(m_old, smax_ref[...])
    alpha = jnp.exp2(m_old - m_new)
    e = jnp.exp2(s_ref[...] - m_new).astype(BF16)
    for p, v_t in enumerate(v_ts):
        cols = slice(p * width, (p + 1) * width)
        pv = _dot(v_t, e[:, cols])
        acc_ref[:, cols] = alpha[:, cols] * acc_ref[:, cols] + pv[0:LANES]
        l_ref[:, cols] = alpha[:, cols] * l_ref[:, cols] + pv[LANES:LANES + 1]
    m_ref[...] = m_new


def _transpose_values(v_ref, vt_ref):
    n_pair, n_chunk, _, tk = vt_ref.shape

    def one_chunk(c, carry):
        rows = pl.ds(pl.multiple_of(c * tk, tk), tk)
        for p in range(n_pair):
            vt_ref[p, c, 0:LANES, :] = v_ref[0, rows, p * LANES:(p + 1) * LANES].astype(F32).T.astype(BF16)
            vt_ref[p, c, LANES:LANES + ONES_ROWS, :] = jnp.ones((ONES_ROWS, tk), BF16)
        return carry

    lax.fori_loop(0, n_chunk, one_chunk, 0)


def _diff_kernel(lq1_ref, lk1_ref, lq2_ref, lk2_ref, subln_ref, q_ref, k_ref, v_ref, o_ref,
                 vt_ref, q4t_ref, s0_ref, s1_ref, smax0_ref, smax1_ref, m_ref, l_ref, acc_ref, *, tq, tk, lam_init):
    i = pl.program_id(1)
    n_pair = vt_ref.shape[0]
    width = 4 * tq
    pair_cols = [slice(p * width, (p + 1) * width) for p in range(n_pair)]
    pair_dims = [slice(p * LANES, (p + 1) * LANES) for p in range(n_pair)]

    @pl.when(i == 0)
    def _():
        _transpose_values(v_ref, vt_ref)

    dim = lax.broadcasted_iota(jnp.int32, (LANES, 1), 0)
    for p in range(n_pair):
        q_t = q_ref[0, :, pair_dims[p]].astype(F32).T
        for n in range(4):
            keep = (dim >= n * DIFF_HALF) & (dim < (n + 1) * DIFF_HALF)
            q4t_ref[:, p * width + n * tq:p * width + (n + 1) * tq] = jnp.where(keep, q_t, 0.0).astype(BF16)
    m_ref[...] = jnp.full(m_ref.shape, MASKED, F32)
    l_ref[...] = jnp.zeros(l_ref.shape, F32)
    acc_ref[...] = jnp.zeros(acc_ref.shape, F32)

    def scores(j, s_ref, masked):
        rows = pl.ds(pl.multiple_of(j * tk, tk), tk)
        if masked:
            key_pos = lax.broadcasted_iota(jnp.int32, (tk, width), 0) + j * tk
            query_pos = (lax.broadcasted_iota(jnp.int32, (tk, width), 1) & (tq - 1)) + i * tq
            causal = key_pos <= query_pos
        for p in range(n_pair):
            s_t = _dot(k_ref[0, rows, pair_dims[p]], q4t_ref[:, pair_cols[p]])
            _store_scores(jnp.where(causal, s_t, -jnp.inf) if masked else s_t, s_ref, pair_cols[p])

    def update(j, s_ref):
        _softmax_update_t(s_ref, [vt_ref[p, j] for p in range(n_pair)], m_ref, l_ref, acc_ref)

    s0_ref, s1_ref = (s0_ref, smax0_ref), (s1_ref, smax1_ref)

    n_full = (i * tq) // tk
    scores(n_full, s0_ref, True)

    def chunk_pair(u, carry):
        scores(2 * u, s1_ref, False)
        update(jnp.where(u == 0, n_full, 2 * u - 1), s0_ref)
        scores(2 * u + 1, s0_ref, False)
        update(2 * u, s1_ref)
        return carry

    n_pairs = n_full // 2
    lax.fori_loop(0, n_pairs, chunk_pair, 0)
    pending = jnp.where(n_pairs == 0, n_full, 2 * n_pairs - 1)

    @pl.when(n_full % 2 == 1)
    def _():
        scores(n_full - 1, s1_ref, False)
        update(pending, s0_ref)
        update(n_full - 1, s1_ref)

    @pl.when(n_full % 2 == 0)
    def _():
        update(pending, s0_ref)

    lam = (jnp.exp(jnp.sum(lq1_ref[...] * lk1_ref[...], axis=-1, keepdims=True))
           - jnp.exp(jnp.sum(lq2_ref[...] * lk2_ref[...], axis=-1, keepdims=True)) + lam_init)
    first = dim < HEAD_DIM
    for p in range(n_pair):
        o4 = acc_ref[:, pair_cols[p]] / l_ref[:, pair_cols[p]]
        o = jnp.where(first, o4[:, 0:tq] - lam * o4[:, tq:2 * tq], o4[:, 2 * tq:3 * tq] - lam * o4[:, 3 * tq:4 * tq])
        sq = o * o
        ms0 = jnp.sum(jnp.where(first, sq, 0.0), axis=0, keepdims=True) * (1.0 / HEAD_DIM)
        ms1 = jnp.sum(jnp.where(first, 0.0, sq), axis=0, keepdims=True) * (1.0 / HEAD_DIM)
        o = o * lax.rsqrt(jnp.where(first, ms0, ms1) + EPS) * subln_ref[...] * (1.0 - lam_init)
        o_ref[0, :, pair_dims[p]] = o.T.astype(o_ref.dtype)


def _diff_attention(qkv, lam_vecs, subln, lam_init, tq=256, tk=512):
    batch, seq, _ = qkv.shape
    n_pair = W_DIFF // LANES
    vec = _resident((1, DIFF_HALF))
    kernel = functools.partial(_diff_kernel, tq=tq, tk=tk, lam_init=lam_init)
    lanes = n_pair * 4 * tq
    return pl.pallas_call(
        kernel,
        grid=(batch, seq // tq),
        in_specs=[vec, vec, vec, vec, _resident((LANES, 1)),
                  pl.BlockSpec((1, tq, W_DIFF), lambda b, i: (b, i, 0)),
                  pl.BlockSpec((1, seq, W_DIFF), lambda b, i: (b, 0, 1)),
                  pl.BlockSpec((1, seq, W_DIFF), lambda b, i: (b, 0, 2))],
        out_specs=pl.BlockSpec((1, tq, W_DIFF), lambda b, i: (b, i, 0)),
        out_shape=jax.ShapeDtypeStruct((batch, seq, W_DIFF), BF16),
        scratch_shapes=[pltpu.VMEM((n_pair, seq // tk, LANES + ONES_ROWS, tk), BF16),
                        pltpu.VMEM((LANES, lanes), BF16),
                        pltpu.VMEM((tk, lanes), F32), pltpu.VMEM((tk, lanes), F32),
                        pltpu.VMEM((1, lanes), F32), pltpu.VMEM((1, lanes), F32),
                        pltpu.VMEM((1, lanes), F32), pltpu.VMEM((1, lanes), F32),
                        pltpu.VMEM((LANES, lanes), F32)],
        compiler_params=_params(2),
        name="diff_attention",
    )(*[v.reshape(1, DIFF_HALF) for v in lam_vecs], jnp.tile(subln, 2).reshape(LANES, 1), qkv, qkv, qkv)


def _dil_kernel(q_ref, k_ref, v_ref, o_ref, o0_ref, o1_ref, o2_ref, l0_ref, l1_ref, l2_ref, *, tile, group):
    base = pl.program_id(2) * tile
    w = DIL_STEPS
    lane = _lane_iota()
    qi = lax.broadcasted_iota(jnp.int32, (2 * w, 2 * w), 0) & (w - 1)
    kj = lax.broadcasted_iota(jnp.int32, (2 * w, 2 * w), 1)
    band = (kj >= qi) & (kj <= qi + w)

    for (window, d), ob_ref, lse_ref in zip(DILATED_PAIRS, (o0_ref, o1_ref, o2_ref), (l0_ref, l1_ref, l2_ref)):
        def rows(ref, start, d=d):
            if d == 1:
                return ref[0, pl.ds(start, w), :]
            return ref[0, pl.ds(start, w, stride=d), :]

        def block_group(grp, carry, window=window, d=d, ob_ref=ob_ref, lse_ref=lse_ref, rows=rows):
            local, prev, own, lo, s, e, m, o = [], [], [], [], [], [], [], []
            for g in range(group):
                blk = grp * group + g
                local.append((blk // d) * window + (blk & (d - 1)))
                own.append(base + local[g])
                has_prev = own[g] >= window
                prev.append(jnp.where(has_prev, own[g] - window, own[g]))
                lo.append(jnp.where(has_prev, 0, w))
                q = rows(q_ref, local[g])
                q2 = jnp.concatenate([jnp.where(lane < HEAD_DIM, q, 0.0), jnp.where(lane < HEAD_DIM, 0.0, q)], axis=0)
                k2 = jnp.concatenate([rows(k_ref, prev[g]), rows(k_ref, own[g])], axis=0)
                s.append(_dot_nt(q2.astype(BF16), k2.astype(BF16)))
            for g in range(group):
                sg = jnp.where(band & (kj >= lo[g]), s[g], -jnp.inf)
                m.append(jnp.max(sg, axis=-1, keepdims=True))
                e.append(jnp.exp2(sg - m[g]).astype(BF16))
            for g in range(group):
                v2 = jnp.concatenate([rows(v_ref, prev[g]), rows(v_ref, own[g])], axis=0).astype(BF16)
                o.append(_dot(e[g], jnp.concatenate([v2, jnp.ones_like(v2)], axis=1)))
            for g in range(group):
                l = o[g][:, LANES:2 * LANES]
                og = o[g][:, 0:LANES] / l
                lse = m[g] + jnp.log2(l)
                ob = jnp.where(lane < HEAD_DIM, og[0:w], og[w:2 * w])
                lb = jnp.where(lane < HEAD_DIM, lse[0:w], lse[w:2 * w])
                if d == 1:
                    ob_ref[pl.ds(local[g], w), :] = ob
                    lse_ref[pl.ds(local[g], w), :] = lb
                else:
                    ob_ref[pl.ds(local[g], w, stride=d), :] = ob
                    lse_ref[pl.ds(local[g], w, stride=d), :] = lb
            return carry

        lax.fori_loop(0, tile // w // group, block_group, 0)

    lse = [l0_ref[...], l1_ref[...], l2_ref[...]]
    m = jnp.maximum(jnp.maximum(lse[0], lse[1]), lse[2])
    wts = [jnp.exp2(x - m) for x in lse]
    mixed = wts[0] * o0_ref[...] + wts[1] * o1_ref[...] + wts[2] * o2_ref[...]
    o_ref[0] = (mixed / (wts[0] + wts[1] + wts[2])).astype(o_ref.dtype)


def _dilated_attention(qkv, group=8):
    batch, seq, _ = qkv.shape
    n_pairs = W_DIL // LANES
    tile = max(window for window, _ in DILATED_PAIRS)
    assert seq % tile == 0
    return pl.pallas_call(
        functools.partial(_dil_kernel, tile=tile, group=group),
        grid=(batch, n_pairs, seq // tile),
        in_specs=[pl.BlockSpec((1, tile, LANES), lambda b, p, i: (b, i, p)),
                  pl.BlockSpec((1, seq, LANES), lambda b, p, i: (b, 0, n_pairs + p)),
                  pl.BlockSpec((1, seq, LANES), lambda b, p, i: (b, 0, 2 * n_pairs + p))],
        out_specs=pl.BlockSpec((1, tile, LANES), lambda b, p, i: (b, i, p)),
        out_shape=jax.ShapeDtypeStruct((batch, seq, W_DIL), BF16),
        scratch_shapes=[pltpu.VMEM((tile, LANES), F32)] * 6,
        compiler_params=_params(3),
        name="dilated_attention",
    )(qkv, qkv, qkv)


def _moba_kernel(q_ref, k_ref, v_ref, o_ref, kaug_ref, vt_ref, kmh_ref, kml_ref, qaugt_ref, so_ref, s0_ref, s1_ref,
                 smaxo_ref, smax0_ref, smax1_ref, m_ref, l_ref, acc_ref):
    i = pl.program_id(1)
    blk = MOBA_BLOCK
    tk = 2 * blk
    blk_shift = blk.bit_length() - 1
    seq = k_ref.shape[1]
    n_blk = seq // blk
    n_pair = vt_ref.shape[0]
    width = 2 * blk
    pair_cols = [slice(p * width, (p + 1) * width) for p in range(n_pair)]
    pair_dims = [slice(p * LANES, (p + 1) * LANES) for p in range(n_pair)]

    @pl.when(i == 0)
    def _():
        qaugt_ref[LANES:2 * LANES, :] = jnp.zeros((LANES, n_pair * width), BF16)
        _transpose_values(v_ref, vt_ref)
        piece = 1024
        for p in range(n_pair):
            def one_piece(c, total, p=p):
                rows = pl.ds(pl.multiple_of(c * piece, piece), piece)
                kc = k_ref[0, rows, pair_dims[p]]
                key_blk = (lax.broadcasted_iota(jnp.int32, (piece, LANES), 0) + c * piece) >> blk_shift
                kaug_ref[p, rows, 0:LANES] = kc
                kaug_ref[p, rows, LANES:2 * LANES] = (
                    lax.broadcasted_iota(jnp.int32, (piece, LANES), 1) == key_blk).astype(BF16)
                member = (lax.broadcasted_iota(jnp.int32, (LANES, piece), 0)
                          == (lax.broadcasted_iota(jnp.int32, (LANES, piece), 1) + c * piece) >> blk_shift)
                return total + _dot(member.astype(BF16), kc)

            total = lax.fori_loop(0, seq // piece, one_piece, jnp.zeros((LANES, LANES), F32))
            k_mean = total * (1.0 / blk)
            hi = k_mean.astype(BF16)
            kmh_ref[p] = hi
            kml_ref[p] = (k_mean - hi.astype(F32)).astype(BF16)

    dim = lax.broadcasted_iota(jnp.int32, (LANES, 1), 0)
    q2t, gates = [], []
    for p in range(n_pair):
        q_t = q_ref[0, :, pair_dims[p]].astype(F32).T
        q2t.append(jnp.concatenate([jnp.where(dim < HEAD_DIM, q_t, 0.0), jnp.where(dim < HEAD_DIM, 0.0, q_t)],
                                   axis=1).astype(BF16))
        gates.append(_dot(kmh_ref[p, 0:n_blk, :], q2t[p]) + _dot(kml_ref[p, 0:n_blk, :], q2t[p]))

    gate = jnp.concatenate(gates, axis=1)
    blk_id = lax.broadcasted_iota(jnp.int32, gate.shape, 0)
    blk_f = blk_id.astype(F32)
    g = jnp.where(blk_id < i, gate, -jnp.inf)
    selected = jnp.zeros(gate.shape, F32)
    for _ in range(MOBA_TOPK):
        best = jnp.max(g, axis=0, keepdims=True)
        first = jnp.min(jnp.where(g == best, blk_f, float(LANES)), axis=0, keepdims=True)
        pick = (blk_f == first) & (best > -jnp.inf)
        selected = jnp.where(pick, 1.0, selected)
        g = jnp.where(pick, -jnp.inf, g)
    for p in range(n_pair):
        qaugt_ref[0:LANES, pair_cols[p]] = q2t[p]
    qaugt_ref[LANES:LANES + n_blk, :] = jnp.where(selected > 0.0, 0.0, MASKED).astype(BF16)

    m_ref[...] = jnp.full(m_ref.shape, MASKED, F32)
    l_ref[...] = jnp.zeros(l_ref.shape, F32)
    acc_ref[...] = jnp.zeros(acc_ref.shape, F32)

    def scores(c, s_ref):
        rows = pl.ds(pl.multiple_of(c * tk, tk), tk)
        for p in range(n_pair):
            _store_scores(_dot(kaug_ref[p, rows, :], qaugt_ref[:, pair_cols[p]]), s_ref, pair_cols[p])

    def update(c, s_ref):
        v_ts = [jnp.concatenate([vt_ref[p, 2 * c], vt_ref[p, 2 * c + 1]], axis=1) for p in range(n_pair)]
        _softmax_update_t(s_ref, v_ts, m_ref, l_ref, acc_ref)

    so_ref, s0_ref, s1_ref = (so_ref, smaxo_ref), (s0_ref, smax0_ref), (s1_ref, smax1_ref)

    own_rows = pl.ds(pl.multiple_of(i * blk, blk), blk)
    key = lax.broadcasted_iota(jnp.int32, (blk, width), 0)
    query = lax.broadcasted_iota(jnp.int32, (blk, width), 1) & (blk - 1)
    for p in range(n_pair):
        s_own = _dot(kaug_ref[p, own_rows, 0:LANES], q2t[p])
        _store_scores(jnp.where(key <= query, s_own, -jnp.inf), so_ref, pair_cols[p])
    scores(0, s0_ref)
    _softmax_update_t(so_ref, [vt_ref[p, i] for p in range(n_pair)], m_ref, l_ref, acc_ref)

    n = jnp.maximum((i + 1) // 2, 1)

    def chunk_pair(u, carry):
        scores(2 * u + 1, s1_ref)
        update(2 * u, s0_ref)
        scores(2 * u + 2, s0_ref)
        update(2 * u + 1, s1_ref)
        return carry

    lax.fori_loop(0, (n - 1) // 2, chunk_pair, 0)

    @pl.when(n % 2 == 1)
    def _():
        update(n - 1, s0_ref)

    @pl.when(n % 2 == 0)
    def _():
        scores(n - 1, s1_ref)
        update(n - 2, s0_ref)
        update(n - 1, s1_ref)

    for p in range(n_pair):
        o = acc_ref[:, pair_cols[p]] / l_ref[:, pair_cols[p]]
        o_ref[0, :, pair_dims[p]] = jnp.where(dim < HEAD_DIM, o[:, 0:blk], o[:, blk:2 * blk]).T.astype(o_ref.dtype)


def _moba_attention(qkv):
    batch, seq, _ = qkv.shape
    n_pair = W_MOBA // LANES
    blk = MOBA_BLOCK
    lanes = n_pair * 2 * blk
    assert seq % 1024 == 0 and seq // blk <= LANES and seq // blk >= MOBA_TOPK
    assert (seq // blk) % 16 == 0
    whole = lambda j: pl.BlockSpec((1, seq, W_MOBA), lambda b, i: (b, 0, j), pipeline_mode=pl.Buffered(1))
    return pl.pallas_call(
        _moba_kernel,
        grid=(batch, seq // blk),
        in_specs=[pl.BlockSpec((1, blk, W_MOBA), lambda b, i: (b, i, 0)), whole(1), whole(2)],
        out_specs=pl.BlockSpec((1, blk, W_MOBA), lambda b, i: (b, i, 0)),
        out_shape=jax.ShapeDtypeStruct((batch, seq, W_MOBA), BF16),
        scratch_shapes=[pltpu.VMEM((n_pair, seq, 2 * LANES), BF16),
                        pltpu.VMEM((n_pair, seq // blk, LANES + ONES_ROWS, blk), BF16),
                        pltpu.VMEM((n_pair, LANES, LANES), BF16), pltpu.VMEM((n_pair, LANES, LANES), BF16),
                        pltpu.VMEM((2 * LANES, lanes), BF16), pltpu.VMEM((blk, lanes), F32),
                        pltpu.VMEM((2 * blk, lanes), F32), pltpu.VMEM((2 * blk, lanes), F32),
                        pltpu.VMEM((1, lanes), F32), pltpu.VMEM((1, lanes), F32), pltpu.VMEM((1, lanes), F32),
                        pltpu.VMEM((1, lanes), F32), pltpu.VMEM((1, lanes), F32),
                        pltpu.VMEM((LANES, lanes), F32)],
        compiler_params=_params(2),
        name="moba_attention",
    )(qkv, qkv, qkv)


def _mem_kv_kernel(mem_ref, nw_ref, w_ref, g64_ref, gain_ref, k_ref, v_ref):
    h = _rms(mem_ref[0], nw_ref[...]).astype(BF16)
    kv = _dot(h, w_ref[...])
    k = kv[:, 0:MEM_WIDTH]
    ms = _dot((k * k).astype(BF16), g64_ref[...])
    k_ref[0] = (k * lax.rsqrt(ms + EPS) * gain_ref[...]).astype(BF16)
    v_ref[0] = kv[:, MEM_WIDTH:2 * MEM_WIDTH].astype(BF16)


def _mem_kv(mem, norm_w, w_mkv, k_gain):
    batch, mem_len, _ = mem.shape
    out = jax.ShapeDtypeStruct((batch, mem_len, MEM_WIDTH), BF16)
    return pl.pallas_call(
        _mem_kv_kernel,
        grid=(batch,),
        in_specs=[pl.BlockSpec((1, mem_len, D_MODEL), lambda b: (b, 0, 0)), _resident((1, D_MODEL)),
                  _resident((D_MODEL, 2 * MEM_WIDTH)), _resident((MXU_DIM, MXU_DIM)), _resident((1, MEM_WIDTH))],
        out_specs=[pl.BlockSpec((1, mem_len, MEM_WIDTH), lambda b: (b, 0, 0))] * 2,
        out_shape=[out, out],
        compiler_params=_params(1),
        name="mem_kv",
    )(mem, norm_w.reshape(1, D_MODEL), w_mkv.astype(BF16), _group_mean_matrix(HEAD_DIM),
      jnp.tile(k_gain, N_HEADS_MEM).reshape(1, MEM_WIDTH))


def _post_kernel(x_ref, a_ref, b_ref, c_ref, wo_ref, ncross_ref, wmq_ref, g64_ref, qgain_ref, km_ref, vm_ref,
                 wmo_ref, nffn_ref, wgu_ref, wd_ref, o_ref, *, tm):
    mix = jnp.concatenate([a_ref[...], b_ref[...], c_ref[...]], axis=1)
    x = x_ref[...] + _dot(mix, wo_ref[...])

    q = _dot(_rms(x, ncross_ref[...]).astype(BF16), wmq_ref[...])
    ms = _dot((q * q).astype(BF16), g64_ref[...])
    qn = (q * lax.rsqrt(ms + EPS) * qgain_ref[...]).astype(BF16)
    head = _lane_iota(MEM_WIDTH) >> (HEAD_DIM.bit_length() - 1)
    q4 = jnp.concatenate([jnp.where(head == n, qn, jnp.zeros_like(qn)) for n in range(N_HEADS_MEM)], axis=0)
    s = _dot_nt(q4, km_ref[0])
    e = jnp.exp(s - jnp.max(s, axis=-1, keepdims=True))
    o4 = _dot(e.astype(BF16), vm_ref[0]) / jnp.sum(e, axis=-1, keepdims=True)
    o = jnp.where(head == 0, o4[0:tm], 0.0)
    for n in range(1, N_HEADS_MEM):
        o = o + jnp.where(head == n, o4[n * tm:(n + 1) * tm], 0.0)
    x = x + _dot(o.astype(BF16), wmo_ref[...])

    h = _rms(x, nffn_ref[...]).astype(BF16)
    y = jnp.zeros((tm, D_MODEL), F32)
    for c in range(D_FF // MXU_DIM):
        lo = c * MXU_DIM
        g = _dot(h, wgu_ref[:, lo:lo + MXU_DIM])
        u = _dot(h, wgu_ref[:, D_FF + lo:D_FF + lo + MXU_DIM])
        y = y + _dot((g * jax.nn.sigmoid(g) * u).astype(BF16), wd_ref[lo:lo + MXU_DIM, :])
    o_ref[...] = x + y


def _post_attention(x2d, outs, seq, w_out, norm_cross, w_mq, q_gain, k_mem, v_mem, w_mo, norm_ffn,
                    w_gate_up, w_down, tm=512):
    rows = x2d.shape[0]
    mem_len = k_mem.shape[1]
    tiles_per_seq = seq // tm
    row_spec = lambda width: pl.BlockSpec((tm, width), lambda i: (i, 0))
    mem_spec = pl.BlockSpec((1, mem_len, MEM_WIDTH), lambda i: (i // tiles_per_seq, 0, 0))
    scale = HEAD_DIM ** -0.5
    return pl.pallas_call(
        functools.partial(_post_kernel, tm=tm),
        grid=(rows // tm,),
        in_specs=[row_spec(D_MODEL), row_spec(W_DIFF), row_spec(W_DIL), row_spec(W_MOBA),
                  _resident((MIX_WIDTH, D_MODEL)), _resident((1, D_MODEL)), _resident((D_MODEL, MEM_WIDTH)),
                  _resident((MXU_DIM, MXU_DIM)), _resident((1, MEM_WIDTH)), mem_spec, mem_spec,
                  _resident((MEM_WIDTH, D_MODEL)), _resident((1, D_MODEL)), _resident((D_MODEL, 2 * D_FF)),
                  _resident((D_FF, D_MODEL))],
        out_specs=row_spec(D_MODEL),
        out_shape=jax.ShapeDtypeStruct((rows, D_MODEL), F32),
        compiler_params=_params(1),
        name="post_attention",
    )(x2d, *[o.reshape(rows, -1) for o in outs], w_out.astype(BF16), norm_cross.reshape(1, D_MODEL),
      w_mq.astype(BF16), _group_mean_matrix(HEAD_DIM), (jnp.tile(q_gain, N_HEADS_MEM) * scale).reshape(1, MEM_WIDTH),
      k_mem, v_mem, w_mo.astype(BF16), norm_ffn.reshape(1, D_MODEL), w_gate_up.astype(BF16), w_down.astype(BF16))


def kernel(x, mem, positions, norm_mix, w_in, qn_diff, kn_diff, lambda_q1, lambda_k1, lambda_q2, lambda_k2,
           subln_diff, qn_dil, kn_dil, qn_moba, kn_moba, w_out, norm_cross, norm_mem, w_mq, w_mkv, qn_mem,
           kn_mem, w_mo, norm_ffn, w_gate_up, w_down):
    batch, seq, _ = x.shape
    depth = w_in.shape[0]
    rows = batch * seq
    tables = _rope_tables(positions)
    x2d = x.reshape(rows, D_MODEL)
    for l in range(depth):
        gains = jnp.concatenate([
            jnp.tile(qn_diff[l], 2 * N_HEADS_DIFF) * (DIFF_HALF ** -0.5 * LOG2_E),
            jnp.tile(qn_dil[l], N_HEADS_DIL) * (HEAD_DIM ** -0.5 * LOG2_E),
            jnp.tile(qn_moba[l], N_HEADS_MOBA) * (HEAD_DIM ** -0.5 * LOG2_E),
            jnp.tile(kn_diff[l], 2 * N_HEADS_DIFF), jnp.tile(kn_dil[l], N_HEADS_DIL),
            jnp.tile(kn_moba[l], N_HEADS_MOBA)]).reshape(1, 2 * MIX_WIDTH)
        qkv_a, qkv_b, qkv_c = _qkv_project(x2d, norm_mix[l], w_in[l], gains, tables)
        lam_init = 0.8 - 0.6 * math.exp(-0.3 * l)
        o_a = _diff_attention(qkv_a.reshape(batch, seq, -1),
                              (lambda_q1[l], lambda_k1[l], lambda_q2[l], lambda_k2[l]), subln_diff[l], lam_init)
        o_b = _dilated_attention(qkv_b.reshape(batch, seq, -1))
        o_c = _moba_attention(qkv_c.reshape(batch, seq, -1))
        k_mem, v_mem = _mem_kv(mem, norm_mem[l], w_mkv[l], kn_mem[l])
        x2d = _post_attention(x2d, (o_a, o_b, o_c), seq, w_out[l], norm_cross[l], w_mq[l], qn_mem[l], k_mem, v_mem,
                              w_mo[l], norm_ffn[l], w_gate_up[l], w_down[l])
    return x2d.reshape(batch, seq, D_MODEL)
```

```python
import functools
import math

import jax
import jax.numpy as jnp
from jax import lax
from jax.experimental import pallas as pl
from jax.experimental.pallas import tpu as pltpu

D_MODEL = 1024
HEAD_DIM = 64
N_HEADS_DIFF = 4
N_HEADS_DIL = 6
N_HEADS_MOBA = 6
DIFF_HALF = HEAD_DIM // 2
ROPE_THETA = 500000.0
ROPE_FRACTION = 4
DILATED_PAIRS = ((128, 1), (512, 4), (2048, 16))
DIL_STEPS = 128
MOBA_BLOCK = 256
MOBA_TOPK = 3
MEM_LEN = 256
N_HEADS_MEM = 4
MEM_WIDTH = N_HEADS_MEM * HEAD_DIM
D_FF = 2816
EPS = 1e-6

W_DIFF = N_HEADS_DIFF * HEAD_DIM
W_DIL = N_HEADS_DIL * HEAD_DIM
W_MOBA = N_HEADS_MOBA * HEAD_DIM
MIX_WIDTH = W_DIFF + W_DIL + W_MOBA

LANES = 128
MXU_DIM = 256
VMEM_LIMIT_BYTES = 56 * 2**20

MASKED = -1e30
LOG2_E = math.log2(math.e)

BF16 = jnp.bfloat16
F32 = jnp.float32
NT_DIMS = (((1,), (1,)), ((), ()))


def _dot(a, b):
    return jnp.dot(a, b, preferred_element_type=F32)


def _dot_nt(a, b):
    return lax.dot_general(a, b, NT_DIMS, preferred_element_type=F32)


def _params(n_grid):
    return pltpu.CompilerParams(dimension_semantics=("arbitrary",) * n_grid,
                                vmem_limit_bytes=VMEM_LIMIT_BYTES)


def _resident(shape):
    nd = len(shape)
    return pl.BlockSpec(shape, lambda *_: (0,) * nd, pipeline_mode=pl.Buffered(1))


def _layer_of(stacked, layer):
    tail = stacked.shape[1:]
    return pl.BlockSpec((None,) + tail, lambda *_: (layer,) + (0,) * len(tail), pipeline_mode=pl.Buffered(1))


def _rms(x, w):
    return x * lax.rsqrt(jnp.mean(x * x, axis=-1, keepdims=True) + EPS) * w


def _lane_iota(n=LANES):
    return lax.broadcasted_iota(jnp.int32, (1, n), 1)


def _rope_kernel(pos_ref, freq_ref, ch_ref, sh_ref, cd_ref, sd_ref):
    pos = pos_ref[...].astype(F32)
    lane = _lane_iota()
    for row, period, c_ref, s_ref in ((0, HEAD_DIM, ch_ref, sh_ref), (1, DIFF_HALF, cd_ref, sd_ref)):
        rot = period // ROPE_FRACTION
        within = lane & (period - 1)
        ang = pos * freq_ref[row:row + 1, :]
        sin = jnp.sin(ang)
        c_ref[...] = jnp.where(within < rot, jnp.cos(ang), 1.0)
        s_ref[...] = jnp.where(within < rot // 2, -sin, jnp.where(within < rot, sin, 0.0))


def _rope_tables(positions):
    rows = positions.size
    tm = 2048
    lane = jnp.arange(LANES)
    freqs = []
    for period in (HEAD_DIM, DIFF_HALF):
        half = period // ROPE_FRACTION // 2
        inv_freq = ROPE_THETA ** (-jnp.arange(half, dtype=F32) / half)
        freqs.append(inv_freq[(lane % period) % half])
    table = jax.ShapeDtypeStruct((rows, LANES), F32)
    return pl.pallas_call(
        _rope_kernel,
        grid=(rows // tm,),
        in_specs=[pl.BlockSpec((tm, 1), lambda i: (i, 0)), pl.BlockSpec((2, LANES), lambda i: (0, 0))],
        out_specs=[pl.BlockSpec((tm, LANES), lambda i: (i, 0))] * 4,
        out_shape=[table] * 4,
        compiler_params=_params(1),
        name="rope_tables",
    )(positions.reshape(rows, 1), jnp.stack(freqs))


def _group_mean_matrix(group):
    idx = jnp.arange(MXU_DIM) // group
    return jnp.where(idx[:, None] == idx[None, :], 1.0 / group, 0.0).astype(BF16)


def _rotate(t, cos, sin, half, period):
    first = (_lane_iota() & (period - 1)) < half
    partner = jnp.where(first, pltpu.roll(t, LANES - half, 1), pltpu.roll(t, half, 1))
    return t * cos + partner * sin


def _qkv_kernel(x_ref, nw_ref, w_ref, gain_ref, g32_ref, g64_ref, ch_ref, sh_ref, cd_ref, sd_ref,
                a_ref, b_ref, c_ref):
    h = _rms(x_ref[...], nw_ref[...]).astype(BF16)
    wide = 2 * MXU_DIM
    for chunk in range(3 * MIX_WIDTH // MXU_DIM):
        section, col0 = divmod(chunk * MXU_DIM, MIX_WIDTH)
        if chunk % 2 == 0:
            y_wide = _dot(h, w_ref[:, chunk * MXU_DIM:chunk * MXU_DIM + wide])
        y = y_wide[:, (chunk % 2) * MXU_DIM:(chunk % 2 + 1) * MXU_DIM]
        if section < 2:
            is_diff = col0 < W_DIFF
            g_ref = g32_ref if is_diff else g64_ref
            ms = _dot((y * y).astype(BF16), g_ref[...])
            y = y * lax.rsqrt(ms + EPS) * gain_ref[:, chunk * MXU_DIM:(chunk + 1) * MXU_DIM]
        for piece in range(MXU_DIM // LANES):
            col = col0 + piece * LANES
            t = y[:, piece * LANES:(piece + 1) * LANES]
            if section < 2:
                if col < W_DIFF:
                    t = _rotate(t, cd_ref[...], sd_ref[...], DIFF_HALF // ROPE_FRACTION // 2, DIFF_HALF)
                else:
                    t = _rotate(t, ch_ref[...], sh_ref[...], HEAD_DIM // ROPE_FRACTION // 2, HEAD_DIM)
            if col < W_DIFF:
                dst = section * W_DIFF + col
                a_ref[:, dst:dst + LANES] = t.astype(BF16)
            elif col < W_DIFF + W_DIL:
                dst = section * W_DIL + col - W_DIFF
                b_ref[:, dst:dst + LANES] = t
            else:
                dst = section * W_MOBA + col - W_DIFF - W_DIL
                c_ref[:, dst:dst + LANES] = t.astype(BF16)


def _qkv_project(x2d, norm_w, w_in, layer, gains, tables, tm=512):
    rows = x2d.shape[0]
    row_spec = lambda width: pl.BlockSpec((tm, width), lambda i: (i, 0))
    return pl.pallas_call(
        _qkv_kernel,
        grid=(rows // tm,),
        in_specs=[row_spec(D_MODEL), _resident((1, D_MODEL)), _layer_of(w_in, layer),
                  _resident((1, 2 * MIX_WIDTH)), _resident((MXU_DIM, MXU_DIM)), _resident((MXU_DIM, MXU_DIM))]
                 + [row_spec(LANES)] * 4,
        out_specs=[row_spec(3 * W_DIFF), row_spec(3 * W_DIL), row_spec(3 * W_MOBA)],
        out_shape=[jax.ShapeDtypeStruct((rows, 3 * W_DIFF), BF16),
                   jax.ShapeDtypeStruct((rows, 3 * W_DIL), F32),
                   jax.ShapeDtypeStruct((rows, 3 * W_MOBA), BF16)],
        compiler_params=_params(1),
        name="qkv_project",
    )(x2d, norm_w.reshape(1, D_MODEL), w_in, gains,
      _group_mean_matrix(DIFF_HALF), _group_mean_matrix(HEAD_DIM), *tables)


def _reduce_rows(x, pair_op, final_op):
    n = x.shape[0]
    while n > 8 and n % 2 == 0:
        n //= 2
        x = pair_op(x[:n], x[n:])
    return final_op(x, axis=0, keepdims=True)


ONES_ROWS = 16


def _store_scores(s_t, buf, cols):
    s_ref, smax_ref = buf
    s_ref[:, cols] = s_t
    smax_ref[:, cols] = _reduce_rows(s_t, jnp.maximum, jnp.max)


def _softmax_update_t(buf, v_ts, m_ref, l_ref, acc_ref):
    s_ref, smax_ref = buf
    width = s_ref.shape[1] // len(v_ts)
    m_old = m_ref[...]
    m_new = jnp.maximum(m_old, smax_ref[...])
    alpha = jnp.exp2(m_old - m_new)
    e = jnp.exp2(s_ref[...] - m_new).astype(BF16)
    for h, v_t in enumerate(v_ts):
        cols = slice(h * width, (h + 1) * width)
        pv = _dot(v_t, e[:, cols])
        acc_ref[:, cols] = alpha[:, cols] * acc_ref[:, cols] + pv[0:HEAD_DIM]
        l_ref[:, cols] = alpha[:, cols] * l_ref[:, cols] + pv[HEAD_DIM:HEAD_DIM + 1]
    m_ref[...] = m_new


def _transpose_values(v_ref, vt_ref):
    n_head, n_chunk, _, tk = vt_ref.shape

    def one_chunk(c, carry):
        rows = pl.ds(pl.multiple_of(c * tk, tk), tk)
        for p in range(n_head // 2):
            pair_t = v_ref[0, rows, p * LANES:(p + 1) * LANES].astype(F32).T.astype(BF16)
            for h in range(2):
                vt_ref[2 * p + h, c, 0:HEAD_DIM, :] = pair_t[h * HEAD_DIM:(h + 1) * HEAD_DIM]
                vt_ref[2 * p + h, c, HEAD_DIM:HEAD_DIM + ONES_ROWS, :] = jnp.ones((ONES_ROWS, tk), BF16)
        return carry

    lax.fori_loop(0, n_chunk, one_chunk, 0)


def _diff_kernel(lq1_ref, lk1_ref, lq2_ref, lk2_ref, subln_ref, q_ref, k_ref, v_ref, o_ref,
                 vt_ref, q4t_ref, s0_ref, s1_ref, smax0_ref, smax1_ref, m_ref, l_ref, acc_ref, *, tq, tk, lam_init):
    i = pl.program_id(1)
    n_pair = vt_ref.shape[0] // 2
    width = 4 * tq
    pair_cols = [slice(p * width, (p + 1) * width) for p in range(n_pair)]
    pair_dims = [slice(p * LANES, (p + 1) * LANES) for p in range(n_pair)]

    @pl.when(i == 0)
    def _():
        _transpose_values(v_ref, vt_ref)

    dim = lax.broadcasted_iota(jnp.int32, (LANES, 1), 0)
    for p in range(n_pair):
        q_t = q_ref[0, :, pair_dims[p]].astype(F32).T
        for n in range(4):
            keep = (dim >= n * DIFF_HALF) & (dim < (n + 1) * DIFF_HALF)
            q4t_ref[:, p * width + n * tq:p * width + (n + 1) * tq] = jnp.where(keep, q_t, 0.0).astype(BF16)
    m_ref[...] = jnp.full(m_ref.shape, MASKED, F32)
    l_ref[...] = jnp.zeros(l_ref.shape, F32)
    acc_ref[...] = jnp.zeros(acc_ref.shape, F32)

    def scores(j, s_ref, masked):
        rows = pl.ds(pl.multiple_of(j * tk, tk), tk)
        if masked:
            key_pos = lax.broadcasted_iota(jnp.int32, (tk, width), 0) + j * tk
            query_pos = (lax.broadcasted_iota(jnp.int32, (tk, width), 1) & (tq - 1)) + i * tq
            causal = key_pos <= query_pos
        for p in range(n_pair):
            s_t = _dot(k_ref[0, rows, pair_dims[p]], q4t_ref[:, pair_cols[p]])
            _store_scores(jnp.where(causal, s_t, -jnp.inf) if masked else s_t, s_ref, pair_cols[p])

    def update(j, s_ref):
        _softmax_update_t(s_ref, [vt_ref[h, j] for h in range(2 * n_pair)], m_ref, l_ref, acc_ref)

    s0_ref, s1_ref = (s0_ref, smax0_ref), (s1_ref, smax1_ref)

    n_full = (i * tq) // tk
    scores(n_full, s0_ref, True)

    def chunk_pair(u, carry):
        scores(2 * u, s1_ref, False)
        update(jnp.where(u == 0, n_full, 2 * u - 1), s0_ref)
        scores(2 * u + 1, s0_ref, False)
        update(2 * u, s1_ref)
        return carry

    n_pairs = n_full // 2
    lax.fori_loop(0, n_pairs, chunk_pair, 0)
    pending = jnp.where(n_pairs == 0, n_full, 2 * n_pairs - 1)

    @pl.when(n_full % 2 == 1)
    def _():
        scores(n_full - 1, s1_ref, False)
        update(pending, s0_ref)
        update(n_full - 1, s1_ref)

    @pl.when(n_full % 2 == 0)
    def _():
        update(pending, s0_ref)

    lam = (jnp.exp(jnp.sum(lq1_ref[...] * lk1_ref[...], axis=-1, keepdims=True))
           - jnp.exp(jnp.sum(lq2_ref[...] * lk2_ref[...], axis=-1, keepdims=True)) + lam_init)
    for p in range(n_pair):
        o4 = acc_ref[:, pair_cols[p]] / l_ref[:, pair_cols[p]]
        heads = []
        for h in range(2):
            o = o4[:, 2 * h * tq:(2 * h + 1) * tq] - lam * o4[:, (2 * h + 1) * tq:(2 * h + 2) * tq]
            heads.append(o * lax.rsqrt(jnp.mean(o * o, axis=0, keepdims=True) + EPS))
        o = jnp.concatenate(heads, axis=0) * subln_ref[...] * (1.0 - lam_init)
        o_ref[0, :, pair_dims[p]] = o.T.astype(o_ref.dtype)


def _diff_attention(qkv, lam_vecs, subln, lam_init, tq=256, tk=512):
    batch, seq, _ = qkv.shape
    n_pair = W_DIFF // LANES
    vec = _resident((1, DIFF_HALF))
    kernel = functools.partial(_diff_kernel, tq=tq, tk=tk, lam_init=lam_init)
    lanes = n_pair * 4 * tq
    return pl.pallas_call(
        kernel,
        grid=(batch, seq // tq),
        in_specs=[vec, vec, vec, vec, _resident((LANES, 1)),
                  pl.BlockSpec((1, tq, W_DIFF), lambda b, i: (b, i, 0)),
                  pl.BlockSpec((1, seq, W_DIFF), lambda b, i: (b, 0, 1)),
                  pl.BlockSpec((1, seq, W_DIFF), lambda b, i: (b, 0, 2))],
        out_specs=pl.BlockSpec((1, tq, W_DIFF), lambda b, i: (b, i, 0)),
        out_shape=jax.ShapeDtypeStruct((batch, seq, W_DIFF), BF16),
        scratch_shapes=[pltpu.VMEM((2 * n_pair, seq // tk, HEAD_DIM + ONES_ROWS, tk), BF16),
                        pltpu.VMEM((LANES, lanes), BF16),
                        pltpu.VMEM((tk, lanes), F32), pltpu.VMEM((tk, lanes), F32),
                        pltpu.VMEM((1, lanes), F32), pltpu.VMEM((1, lanes), F32),
                        pltpu.VMEM((1, lanes), F32), pltpu.VMEM((1, lanes), F32),
                        pltpu.VMEM((HEAD_DIM, lanes), F32)],
        compiler_params=_params(2),
        name="diff_attention",
    )(*[v.reshape(1, DIFF_HALF) for v in lam_vecs], jnp.tile(subln, 2).reshape(LANES, 1), qkv, qkv, qkv)


def _dil_kernel(q_ref, k_ref, v_ref, o_ref, o0_ref, o1_ref, o2_ref, l0_ref, l1_ref, l2_ref, *, tile, group):
    base = pl.program_id(2) * tile
    w = DIL_STEPS
    lane = _lane_iota()
    qi = lax.broadcasted_iota(jnp.int32, (2 * w, 2 * w), 0) & (w - 1)
    kj = lax.broadcasted_iota(jnp.int32, (2 * w, 2 * w), 1)
    band = (kj >= qi) & (kj <= qi + w)

    for (window, d), ob_ref, lse_ref in zip(DILATED_PAIRS, (o0_ref, o1_ref, o2_ref), (l0_ref, l1_ref, l2_ref)):
        def rows(ref, start, d=d):
            if d == 1:
                return ref[0, pl.ds(start, w), :]
            return ref[0, pl.ds(start, w, stride=d), :]

        def block_group(grp, carry, window=window, d=d, ob_ref=ob_ref, lse_ref=lse_ref, rows=rows):
            local, prev, own, lo, s, e, m, o = [], [], [], [], [], [], [], []
            for g in range(group):
                blk = grp * group + g
                local.append((blk // d) * window + (blk & (d - 1)))
                own.append(base + local[g])
                has_prev = own[g] >= window
                prev.append(jnp.where(has_prev, own[g] - window, own[g]))
                lo.append(jnp.where(has_prev, 0, w))
                q = rows(q_ref, local[g])
                q2 = jnp.concatenate([jnp.where(lane < HEAD_DIM, q, 0.0), jnp.where(lane < HEAD_DIM, 0.0, q)], axis=0)
                k2 = jnp.concatenate([rows(k_ref, prev[g]), rows(k_ref, own[g])], axis=0)
                s.append(_dot_nt(q2.astype(BF16), k2.astype(BF16)))
            for g in range(group):
                sg = jnp.where(band & (kj >= lo[g]), s[g], -jnp.inf)
                m.append(jnp.max(sg, axis=-1, keepdims=True))
                e.append(jnp.exp2(sg - m[g]).astype(BF16))
            for g in range(group):
                v2 = jnp.concatenate([rows(v_ref, prev[g]), rows(v_ref, own[g])], axis=0).astype(BF16)
                o.append(_dot(e[g], jnp.concatenate([v2, jnp.ones_like(v2)], axis=1)))
            for g in range(group):
                l = o[g][:, LANES:2 * LANES]
                og = o[g][:, 0:LANES] / l
                lse = m[g] + jnp.log2(l)
                ob = jnp.where(lane < HEAD_DIM, og[0:w], og[w:2 * w])
                lb = jnp.where(lane < HEAD_DIM, lse[0:w], lse[w:2 * w])
                if d == 1:
                    ob_ref[pl.ds(local[g], w), :] = ob
                    lse_ref[pl.ds(local[g], w), :] = lb
                else:
                    ob_ref[pl.ds(local[g], w, stride=d), :] = ob
                    lse_ref[pl.ds(local[g], w, stride=d), :] = lb
            return carry

        lax.fori_loop(0, tile // w // group, block_group, 0)

    lse = [l0_ref[...], l1_ref[...], l2_ref[...]]
    m = jnp.maximum(jnp.maximum(lse[0], lse[1]), lse[2])
    wts = [jnp.exp2(x - m) for x in lse]
    mixed = wts[0] * o0_ref[...] + wts[1] * o1_ref[...] + wts[2] * o2_ref[...]
    o_ref[0] = (mixed / (wts[0] + wts[1] + wts[2])).astype(o_ref.dtype)


def _dilated_attention(qkv, group=8):
    batch, seq, _ = qkv.shape
    n_pairs = W_DIL // LANES
    tile = max(window for window, _ in DILATED_PAIRS)
    assert seq % tile == 0
    return pl.pallas_call(
        functools.partial(_dil_kernel, tile=tile, group=group),
        grid=(batch, n_pairs, seq // tile),
        in_specs=[pl.BlockSpec((1, tile, LANES), lambda b, p, i: (b, i, p)),
                  pl.BlockSpec((1, seq, LANES), lambda b, p, i: (b, 0, n_pairs + p)),
                  pl.BlockSpec((1, seq, LANES), lambda b, p, i: (b, 0, 2 * n_pairs + p))],
        out_specs=pl.BlockSpec((1, tile, LANES), lambda b, p, i: (b, i, p)),
        out_shape=jax.ShapeDtypeStruct((batch, seq, W_DIL), BF16),
        scratch_shapes=[pltpu.VMEM((tile, LANES), F32)] * 6,
        compiler_params=_params(3),
        name="dilated_attention",
    )(qkv, qkv, qkv)


def _moba_kernel(q_ref, k_ref, v_ref, o_ref, kaug_ref, vt_ref, kmh_ref, kml_ref, qaugt_ref, so_ref, s0_ref, s1_ref,
                 smaxo_ref, smax0_ref, smax1_ref, m_ref, l_ref, acc_ref):
    i = pl.program_id(1)
    blk = MOBA_BLOCK
    tk = 2 * blk
    blk_shift = blk.bit_length() - 1
    seq = k_ref.shape[1]
    n_blk = seq // blk
    n_pair = vt_ref.shape[0] // 2
    width = 2 * blk
    pair_cols = [slice(p * width, (p + 1) * width) for p in range(n_pair)]
    pair_dims = [slice(p * LANES, (p + 1) * LANES) for p in range(n_pair)]

    @pl.when(i == 0)
    def _():
        qaugt_ref[LANES:2 * LANES, :] = jnp.zeros((LANES, n_pair * width), BF16)
        _transpose_values(v_ref, vt_ref)
        piece = 1024
        for p in range(n_pair):
            def one_piece(c, total, p=p):
                rows = pl.ds(pl.multiple_of(c * piece, piece), piece)
                kc = k_ref[0, rows, pair_dims[p]]
                key_blk = (lax.broadcasted_iota(jnp.int32, (piece, LANES), 0) + c * piece) >> blk_shift
                kaug_ref[p, rows, 0:LANES] = kc
                kaug_ref[p, rows, LANES:2 * LANES] = (
                    lax.broadcasted_iota(jnp.int32, (piece, LANES), 1) == key_blk).astype(BF16)
                member = (lax.broadcasted_iota(jnp.int32, (LANES, piece), 0)
                          == (lax.broadcasted_iota(jnp.int32, (LANES, piece), 1) + c * piece) >> blk_shift)
                return total + _dot(member.astype(BF16), kc)

            total = lax.fori_loop(0, seq // piece, one_piece, jnp.zeros((LANES, LANES), F32))
            k_mean = total * (1.0 / blk)
            hi = k_mean.astype(BF16)
            kmh_ref[p] = hi
            kml_ref[p] = (k_mean - hi.astype(F32)).astype(BF16)

    dim = lax.broadcasted_iota(jnp.int32, (LANES, 1), 0)
    q2t, gates = [], []
    for p in range(n_pair):
        q_t = q_ref[0, :, pair_dims[p]].astype(F32).T
        q2t.append(jnp.concatenate([jnp.where(dim < HEAD_DIM, q_t, 0.0), jnp.where(dim < HEAD_DIM, 0.0, q_t)],
                                   axis=1).astype(BF16))
        gates.append(_dot(kmh_ref[p, 0:n_blk, :], q2t[p]) + _dot(kml_ref[p, 0:n_blk, :], q2t[p]))

    gate = jnp.concatenate(gates, axis=1)
    blk_id = lax.broadcasted_iota(jnp.int32, gate.shape, 0)
    blk_f = blk_id.astype(F32)
    g = jnp.where(blk_id < i, gate, -jnp.inf)
    selected = jnp.zeros(gate.shape, F32)
    for _ in range(MOBA_TOPK):
        best = jnp.max(g, axis=0, keepdims=True)
        first = jnp.min(jnp.where(g == best, blk_f, float(LANES)), axis=0, keepdims=True)
        pick = (blk_f == first) & (best > -jnp.inf)
        selected = jnp.where(pick, 1.0, selected)
        g = jnp.where(pick, -jnp.inf, g)
    for p in range(n_pair):
        qaugt_ref[0:LANES, pair_cols[p]] = q2t[p]
    qaugt_ref[LANES:LANES + n_blk, :] = jnp.where(selected > 0.0, 0.0, MASKED).astype(BF16)

    m_ref[...] = jnp.full(m_ref.shape, MASKED, F32)
    l_ref[...] = jnp.zeros(l_ref.shape, F32)
    acc_ref[...] = jnp.zeros(acc_ref.shape, F32)

    def scores(c, s_ref):
        rows = pl.ds(pl.multiple_of(c * tk, tk), tk)
        for p in range(n_pair):
            _store_scores(_dot(kaug_ref[p, rows, :], qaugt_ref[:, pair_cols[p]]), s_ref, pair_cols[p])

    def update(c, s_ref):
        v_ts = [jnp.concatenate([vt_ref[h, 2 * c], vt_ref[h, 2 * c + 1]], axis=1) for h in range(2 * n_pair)]
        _softmax_update_t(s_ref, v_ts, m_ref, l_ref, acc_ref)

    so_ref, s0_ref, s1_ref = (so_ref, smaxo_ref), (s0_ref, smax0_ref), (s1_ref, smax1_ref)

    own_rows = pl.ds(pl.multiple_of(i * blk, blk), blk)
    key = lax.broadcasted_iota(jnp.int32, (blk, width), 0)
    query = lax.broadcasted_iota(jnp.int32, (blk, width), 1) & (blk - 1)
    for p in range(n_pair):
        s_own = _dot(kaug_ref[p, own_rows, 0:LANES], q2t[p])
        _store_scores(jnp.where(key <= query, s_own, -jnp.inf), so_ref, pair_cols[p])
    scores(0, s0_ref)
    _softmax_update_t(so_ref, [vt_ref[h, i] for h in range(2 * n_pair)], m_ref, l_ref, acc_ref)

    n = jnp.maximum((i + 1) // 2, 1)

    def chunk_pair(u, carry):
        scores(2 * u + 1, s1_ref)
        update(2 * u, s0_ref)
        scores(2 * u + 2, s0_ref)
        update(2 * u + 1, s1_ref)
        return carry

    lax.fori_loop(0, (n - 1) // 2, chunk_pair, 0)

    @pl.when(n % 2 == 1)
    def _():
        update(n - 1, s0_ref)

    @pl.when(n % 2 == 0)
    def _():
        scores(n - 1, s1_ref)
        update(n - 2, s0_ref)
        update(n - 1, s1_ref)

    for p in range(n_pair):
        o = acc_ref[:, pair_cols[p]] / l_ref[:, pair_cols[p]]
        o_ref[0, :, pair_dims[p]] = jnp.concatenate([o[:, 0:blk], o[:, blk:2 * blk]], axis=0).T.astype(o_ref.dtype)


def _moba_attention(qkv):
    batch, seq, _ = qkv.shape
    n_pair = W_MOBA // LANES
    blk = MOBA_BLOCK
    lanes = n_pair * 2 * blk
    assert seq % 1024 == 0 and seq // blk <= LANES and seq // blk >= MOBA_TOPK
    assert (seq // blk) % 16 == 0
    whole = lambda j: pl.BlockSpec((1, seq, W_MOBA), lambda b, i: (b, 0, j), pipeline_mode=pl.Buffered(1))
    return pl.pallas_call(
        _moba_kernel,
        grid=(batch, seq // blk),
        in_specs=[pl.BlockSpec((1, blk, W_MOBA), lambda b, i: (b, i, 0)), whole(1), whole(2)],
        out_specs=pl.BlockSpec((1, blk, W_MOBA), lambda b, i: (b, i, 0)),
        out_shape=jax.ShapeDtypeStruct((batch, seq, W_MOBA), BF16),
        scratch_shapes=[pltpu.VMEM((n_pair, seq, 2 * LANES), BF16),
                        pltpu.VMEM((2 * n_pair, seq // blk, HEAD_DIM + ONES_ROWS, blk), BF16),
                        pltpu.VMEM((n_pair, LANES, LANES), BF16), pltpu.VMEM((n_pair, LANES, LANES), BF16),
                        pltpu.VMEM((2 * LANES, lanes), BF16), pltpu.VMEM((blk, lanes), F32),
                        pltpu.VMEM((2 * blk, lanes), F32), pltpu.VMEM((2 * blk, lanes), F32),
                        pltpu.VMEM((1, lanes), F32), pltpu.VMEM((1, lanes), F32), pltpu.VMEM((1, lanes), F32),
                        pltpu.VMEM((1, lanes), F32), pltpu.VMEM((1, lanes), F32),
                        pltpu.VMEM((HEAD_DIM, lanes), F32)],
        compiler_params=_params(2),
        name="moba_attention",
    )(qkv, qkv, qkv)


def _mem_kv_kernel(mem_ref, nw_ref, w_ref, g64_ref, gain_ref, k_ref, v_ref):
    h = _rms(mem_ref[0], nw_ref[...]).astype(BF16)
    kv = _dot(h, w_ref[...])
    k = kv[:, 0:MEM_WIDTH]
    ms = _dot((k * k).astype(BF16), g64_ref[...])
    k_ref[0] = (k * lax.rsqrt(ms + EPS) * gain_ref[...]).astype(BF16)
    v_ref[0] = kv[:, MEM_WIDTH:2 * MEM_WIDTH].astype(BF16)


def _mem_kv(mem, norm_w, w_mkv, layer, k_gain):
    batch, mem_len, _ = mem.shape
    out = jax.ShapeDtypeStruct((batch, mem_len, MEM_WIDTH), BF16)
    return pl.pallas_call(
        _mem_kv_kernel,
        grid=(batch,),
        in_specs=[pl.BlockSpec((1, mem_len, D_MODEL), lambda b: (b, 0, 0)), _resident((1, D_MODEL)),
                  _layer_of(w_mkv, layer), _resident((MXU_DIM, MXU_DIM)), _resident((1, MEM_WIDTH))],
        out_specs=[pl.BlockSpec((1, mem_len, MEM_WIDTH), lambda b: (b, 0, 0))] * 2,
        out_shape=[out, out],
        compiler_params=_params(1),
        name="mem_kv",
    )(mem, norm_w.reshape(1, D_MODEL), w_mkv, _group_mean_matrix(HEAD_DIM),
      jnp.tile(k_gain, N_HEADS_MEM).reshape(1, MEM_WIDTH))


def _post_kernel(x_ref, a_ref, b_ref, c_ref, wo_ref, ncross_ref, wmq_ref, g64_ref, qgain_ref, km_ref, vm_ref,
                 wmo_ref, nffn_ref, wgu_ref, wd_ref, o_ref, *, tm):
    mix = jnp.concatenate([a_ref[...], b_ref[...], c_ref[...]], axis=1)
    x = x_ref[...] + _dot(mix, wo_ref[...])

    q = _dot(_rms(x, ncross_ref[...]).astype(BF16), wmq_ref[...])
    ms = _dot((q * q).astype(BF16), g64_ref[...])
    qn = (q * lax.rsqrt(ms + EPS) * qgain_ref[...]).astype(BF16)
    head = _lane_iota(MEM_WIDTH) >> (HEAD_DIM.bit_length() - 1)
    q4 = jnp.concatenate([jnp.where(head == n, qn, jnp.zeros_like(qn)) for n in range(N_HEADS_MEM)], axis=0)
    s = _dot_nt(q4, km_ref[0])
    e = jnp.exp(s - jnp.max(s, axis=-1, keepdims=True))
    o4 = _dot(e.astype(BF16), vm_ref[0]) / jnp.sum(e, axis=-1, keepdims=True)
    o = jnp.where(head == 0, o4[0:tm], 0.0)
    for n in range(1, N_HEADS_MEM):
        o = o + jnp.where(head == n, o4[n * tm:(n + 1) * tm], 0.0)
    x = x + _dot(o.astype(BF16), wmo_ref[...])

    h = _rms(x, nffn_ref[...]).astype(BF16)
    y = jnp.zeros((tm, D_MODEL), F32)
    for c in range(D_FF // MXU_DIM):
        lo = c * MXU_DIM
        g = _dot(h, wgu_ref[:, lo:lo + MXU_DIM])
        u = _dot(h, wgu_ref[:, D_FF + lo:D_FF + lo + MXU_DIM])
        y = y + _dot((g * jax.nn.sigmoid(g) * u).astype(BF16), wd_ref[lo:lo + MXU_DIM, :])
    o_ref[...] = x + y


def _post_attention(x2d, outs, seq, layer, w_out, norm_cross, w_mq, q_gain, k_mem, v_mem, w_mo, norm_ffn,
                    w_gate_up, w_down, tm=512):
    rows = x2d.shape[0]
    mem_len = k_mem.shape[1]
    tiles_per_seq = seq // tm
    row_spec = lambda width: pl.BlockSpec((tm, width), lambda i: (i, 0))
    mem_spec = pl.BlockSpec((1, mem_len, MEM_WIDTH), lambda i: (i // tiles_per_seq, 0, 0))
    scale = HEAD_DIM ** -0.5
    return pl.pallas_call(
        functools.partial(_post_kernel, tm=tm),
        grid=(rows // tm,),
        in_specs=[row_spec(D_MODEL), row_spec(W_DIFF), row_spec(W_DIL), row_spec(W_MOBA),
                  _layer_of(w_out, layer), _resident((1, D_MODEL)), _layer_of(w_mq, layer),
                  _resident((MXU_DIM, MXU_DIM)), _resident((1, MEM_WIDTH)), mem_spec, mem_spec,
                  _layer_of(w_mo, layer), _resident((1, D_MODEL)), _layer_of(w_gate_up, layer),
                  _layer_of(w_down, layer)],
        out_specs=row_spec(D_MODEL),
        out_shape=jax.ShapeDtypeStruct((rows, D_MODEL), F32),
        compiler_params=_params(1),
        name="post_attention",
    )(x2d, *[o.reshape(rows, -1) for o in outs], w_out, norm_cross.reshape(1, D_MODEL),
      w_mq, _group_mean_matrix(HEAD_DIM), (jnp.tile(q_gain, N_HEADS_MEM) * scale).reshape(1, MEM_WIDTH),
      k_mem, v_mem, w_mo, norm_ffn.reshape(1, D_MODEL), w_gate_up, w_down)


def kernel(x, mem, positions, norm_mix, w_in, qn_diff, kn_diff, lambda_q1, lambda_k1, lambda_q2, lambda_k2,
           subln_diff, qn_dil, kn_dil, qn_moba, kn_moba, w_out, norm_cross, norm_mem, w_mq, w_mkv, qn_mem,
           kn_mem, w_mo, norm_ffn, w_gate_up, w_down):
    batch, seq, _ = x.shape
    depth = w_in.shape[0]
    rows = batch * seq
    tables = _rope_tables(positions)
    x2d = x.reshape(rows, D_MODEL)
    w_in, w_out, w_mq, w_mkv, w_mo, w_gate_up, w_down = (
        w.astype(BF16) for w in (w_in, w_out, w_mq, w_mkv, w_mo, w_gate_up, w_down))
    for l in range(depth):
        gains = jnp.concatenate([
            jnp.tile(qn_diff[l], 2 * N_HEADS_DIFF) * (DIFF_HALF ** -0.5 * LOG2_E),
            jnp.tile(qn_dil[l], N_HEADS_DIL) * (HEAD_DIM ** -0.5 * LOG2_E),
            jnp.tile(qn_moba[l], N_HEADS_MOBA) * (HEAD_DIM ** -0.5 * LOG2_E),
            jnp.tile(kn_diff[l], 2 * N_HEADS_DIFF), jnp.tile(kn_dil[l], N_HEADS_DIL),
            jnp.tile(kn_moba[l], N_HEADS_MOBA)]).reshape(1, 2 * MIX_WIDTH)
        qkv_a, qkv_b, qkv_c = _qkv_project(x2d, norm_mix[l], w_in, l, gains, tables)
        lam_init = 0.8 - 0.6 * math.exp(-0.3 * l)
        o_a = _diff_attention(qkv_a.reshape(batch, seq, -1),
                              (lambda_q1[l], lambda_k1[l], lambda_q2[l], lambda_k2[l]), subln_diff[l], lam_init)
        o_b = _dilated_attention(qkv_b.reshape(batch, seq, -1))
        o_c = _moba_attention(qkv_c.reshape(batch, seq, -1))
        k_mem, v_mem = _mem_kv(mem, norm_mem[l], w_mkv, l, kn_mem[l])
        x2d = _post_attention(x2d, (o_a, o_b, o_c), seq, l, w_out, norm_cross[l], w_mq, qn_mem[l], k_mem, v_mem,
                              w_mo, norm_ffn[l], w_gate_up, w_down)
    return x2d.reshape(batch, seq, D_MODEL)
```

```python
import functools
import math

import jax
import jax.numpy as jnp
import numpy as np
from jax import lax
from jax.experimental import pallas as pl
from jax.experimental.pallas import tpu as pltpu

D_MODEL = 1024
HEAD_DIM = 64
N_HEADS_DIFF = 4
N_HEADS_DIL = 6
N_HEADS_MOBA = 6
DIFF_HALF = HEAD_DIM // 2
ROPE_THETA = 500000.0
ROPE_FRACTION = 4
DILATED_PAIRS = ((128, 1), (512, 4), (2048, 16))
DIL_STEPS = 128
MOBA_BLOCK = 256
MOBA_TOPK = 3
MEM_LEN = 256
N_HEADS_MEM = 4
MEM_WIDTH = N_HEADS_MEM * HEAD_DIM
D_FF = 2816
EPS = 1e-6

W_DIFF = N_HEADS_DIFF * HEAD_DIM
W_DIL = N_HEADS_DIL * HEAD_DIM
W_MOBA = N_HEADS_MOBA * HEAD_DIM
MIX_WIDTH = W_DIFF + W_DIL + W_MOBA

LANES = 128
MXU_DIM = 256
VMEM_LIMIT_BYTES = 56 * 2**20

MASKED = -1e30
LOG2_E = math.log2(math.e)

BF16 = jnp.bfloat16
F32 = jnp.float32
NT_DIMS = (((1,), (1,)), ((), ()))


def _dot(a, b):
    return jnp.dot(a, b, preferred_element_type=F32)


def _dot_nt(a, b):
    return lax.dot_general(a, b, NT_DIMS, preferred_element_type=F32)


def _params(n_grid):
    return pltpu.CompilerParams(dimension_semantics=("arbitrary",) * n_grid,
                                vmem_limit_bytes=VMEM_LIMIT_BYTES)


def _resident(shape):
    nd = len(shape)
    return pl.BlockSpec(shape, lambda *_: (0,) * nd, pipeline_mode=pl.Buffered(1))


def _layer_of(stacked, layer):
    tail = stacked.shape[1:]
    return pl.BlockSpec((None,) + tail, lambda *_: (layer,) + (0,) * len(tail), pipeline_mode=pl.Buffered(1))


def _rms(x, w):
    return x * lax.rsqrt(jnp.mean(x * x, axis=-1, keepdims=True) + EPS) * w


def _lane_iota(n=LANES):
    return lax.broadcasted_iota(jnp.int32, (1, n), 1)


N_FREQ = 16


def _rope_kernel(pos_ref, freq_ref, sel_ref, ch_ref, cd_ref, sh_ref, sd_ref):
    tm = pos_ref.shape[1]
    ang = freq_ref[...] * pos_ref[...].astype(F32)
    pad = jnp.zeros((LANES - N_FREQ, tm), F32)
    for trig, tables in ((jnp.cos, ((0, ch_ref), (1, cd_ref))), (jnp.sin, ((2, sh_ref), (3, sd_ref)))):
        x = jnp.concatenate([trig(ang), pad], axis=0).T
        hi = x.astype(BF16)
        rest = x - hi.astype(F32)
        mid = rest.astype(BF16)
        lo = (rest - mid.astype(F32)).astype(BF16)
        for k, out_ref in tables:
            out_ref[...] = _dot(hi, sel_ref[k]) + _dot(mid, sel_ref[k]) + _dot(lo, sel_ref[k])


def _rope_selectors():
    sel = np.zeros((4, LANES, LANES), np.float32)
    one_row = N_FREQ - 1
    row0 = 0
    for kind, period in ((0, HEAD_DIM), (1, DIFF_HALF)):
        half = period // ROPE_FRACTION // 2
        for lane in range(LANES):
            within = lane % period
            if within < 2 * half:
                sel[kind, row0 + within % half, lane] = 1.0
                sel[2 + kind, row0 + within % half, lane] = -1.0 if within < half else 1.0
            else:
                sel[kind, one_row, lane] = 1.0
        row0 += half
    return jnp.asarray(sel, BF16)


def _rope_tables(positions):
    rows = positions.size
    tm = 2048
    freqs = [ROPE_THETA ** (-jnp.arange(half, dtype=F32) / half)
             for half in (HEAD_DIM // ROPE_FRACTION // 2, DIFF_HALF // ROPE_FRACTION // 2)]
    freqs = jnp.concatenate(freqs + [jnp.zeros((N_FREQ - sum(f.size for f in freqs),), F32)]).reshape(N_FREQ, 1)
    table = jax.ShapeDtypeStruct((rows, LANES), F32)
    ch, cd, sh, sd = pl.pallas_call(
        _rope_kernel,
        grid=(rows // tm,),
        in_specs=[pl.BlockSpec((1, tm), lambda i: (0, i)), _resident((N_FREQ, 1)), _resident((4, LANES, LANES))],
        out_specs=[pl.BlockSpec((tm, LANES), lambda i: (i, 0))] * 4,
        out_shape=[table] * 4,
        compiler_params=_params(1),
        name="rope_tables",
    )(positions.reshape(1, rows), freqs, _rope_selectors())
    return ch, sh, cd, sd


def _group_mean_matrix(group):
    idx = jnp.arange(MXU_DIM) // group
    return jnp.where(idx[:, None] == idx[None, :], 1.0 / group, 0.0).astype(BF16)


def _rotate(t, cos, sin, half, period):
    first = (_lane_iota() & (period - 1)) < half
    partner = jnp.where(first, pltpu.roll(t, LANES - half, 1), pltpu.roll(t, half, 1))
    return t * cos + partner * sin


def _qkv_kernel(x_ref, nw_ref, w_ref, gain_ref, g32_ref, g64_ref, ch_ref, sh_ref, cd_ref, sd_ref,
                a_ref, b_ref, c_ref):
    h = _rms(x_ref[...], nw_ref[...]).astype(BF16)
    wide = 2 * MXU_DIM
    for chunk in range(3 * MIX_WIDTH // MXU_DIM):
        section, col0 = divmod(chunk * MXU_DIM, MIX_WIDTH)
        if chunk % 2 == 0:
            y_wide = _dot(h, w_ref[:, chunk * MXU_DIM:chunk * MXU_DIM + wide])
        y = y_wide[:, (chunk % 2) * MXU_DIM:(chunk % 2 + 1) * MXU_DIM]
        if section < 2:
            is_diff = col0 < W_DIFF
            g_ref = g32_ref if is_diff else g64_ref
            ms = _dot((y * y).astype(BF16), g_ref[...])
            y = y * lax.rsqrt(ms + EPS) * gain_ref[:, chunk * MXU_DIM:(chunk + 1) * MXU_DIM]
        for piece in range(MXU_DIM // LANES):
            col = col0 + piece * LANES
            t = y[:, piece * LANES:(piece + 1) * LANES]
            if section < 2:
                if col < W_DIFF:
                    t = _rotate(t, cd_ref[...], sd_ref[...], DIFF_HALF // ROPE_FRACTION // 2, DIFF_HALF)
                else:
                    t = _rotate(t, ch_ref[...], sh_ref[...], HEAD_DIM // ROPE_FRACTION // 2, HEAD_DIM)
            if col < W_DIFF:
                dst = section * W_DIFF + col
                a_ref[:, dst:dst + LANES] = t.astype(BF16)
            elif col < W_DIFF + W_DIL:
                dst = section * W_DIL + col - W_DIFF
                b_ref[:, dst:dst + LANES] = t
            else:
                dst = section * W_MOBA + col - W_DIFF - W_DIL
                c_ref[:, dst:dst + LANES] = t.astype(BF16)


def _qkv_project(x2d, norm_w, w_in, layer, gains, tables, tm=512):
    rows = x2d.shape[0]
    row_spec = lambda width: pl.BlockSpec((tm, width), lambda i: (i, 0))
    return pl.pallas_call(
        _qkv_kernel,
        grid=(rows // tm,),
        in_specs=[row_spec(D_MODEL), _resident((1, D_MODEL)), _layer_of(w_in, layer),
                  _resident((1, 2 * MIX_WIDTH)), _resident((MXU_DIM, MXU_DIM)), _resident((MXU_DIM, MXU_DIM))]
                 + [row_spec(LANES)] * 4,
        out_specs=[row_spec(3 * W_DIFF), row_spec(3 * W_DIL), row_spec(3 * W_MOBA)],
        out_shape=[jax.ShapeDtypeStruct((rows, 3 * W_DIFF), BF16),
                   jax.ShapeDtypeStruct((rows, 3 * W_DIL), F32),
                   jax.ShapeDtypeStruct((rows, 3 * W_MOBA), BF16)],
        compiler_params=_params(1),
        name="qkv_project",
    )(x2d, norm_w.reshape(1, D_MODEL), w_in, gains,
      _group_mean_matrix(DIFF_HALF), _group_mean_matrix(HEAD_DIM), *tables)


def _reduce_rows(x, pair_op, final_op):
    n = x.shape[0]
    while n > 8 and n % 2 == 0:
        n //= 2
        x = pair_op(x[:n], x[n:])
    return final_op(x, axis=0, keepdims=True)


ONES_ROWS = 16


def _store_scores(s_t, buf, cols):
    s_ref, smax_ref = buf
    s_ref[:, cols] = s_t
    smax_ref[:, cols] = _reduce_rows(s_t, jnp.maximum, jnp.max)


def _softmax_update_t(buf, v_ts, m_ref, l_ref, acc_ref):
    s_ref, smax_ref = buf
    width = s_ref.shape[1] // len(v_ts)
    m_old = m_ref[...]
    m_new = jnp.maximum(m_old, smax_ref[...])
    alpha = jnp.exp2(m_old - m_new)
    e = jnp.exp2(s_ref[...] - m_new).astype(BF16)
    for h, v_t in enumerate(v_ts):
        cols = slice(h * width, (h + 1) * width)
        pv = _dot(v_t, e[:, cols])
        acc_ref[:, cols] = alpha[:, cols] * acc_ref[:, cols] + pv[0:HEAD_DIM]
        l_ref[:, cols] = alpha[:, cols] * l_ref[:, cols] + pv[HEAD_DIM:HEAD_DIM + 1]
    m_ref[...] = m_new


def _transpose_values(v_ref, vt_ref):
    n_head, n_chunk, _, tk = vt_ref.shape

    def one_chunk(c, carry):
        rows = pl.ds(pl.multiple_of(c * tk, tk), tk)
        for p in range(n_head // 2):
            pair_t = v_ref[0, rows, p * LANES:(p + 1) * LANES].astype(F32).T.astype(BF16)
            for h in range(2):
                vt_ref[2 * p + h, c, 0:HEAD_DIM, :] = pair_t[h * HEAD_DIM:(h + 1) * HEAD_DIM]
                vt_ref[2 * p + h, c, HEAD_DIM:HEAD_DIM + ONES_ROWS, :] = jnp.ones((ONES_ROWS, tk), BF16)
        return carry

    lax.fori_loop(0, n_chunk, one_chunk, 0)


def _diff_kernel(lq1_ref, lk1_ref, lq2_ref, lk2_ref, subln_ref, q_ref, k_ref, v_ref, o_ref,
                 vt_ref, q4t_ref, s0_ref, s1_ref, smax0_ref, smax1_ref, m_ref, l_ref, acc_ref, *, tq, tk, lam_init):
    i = pl.program_id(1)
    n_pair = vt_ref.shape[0] // 2
    width = 4 * tq
    pair_cols = [slice(p * width, (p + 1) * width) for p in range(n_pair)]
    pair_dims = [slice(p * LANES, (p + 1) * LANES) for p in range(n_pair)]

    @pl.when(i == 0)
    def _():
        _transpose_values(v_ref, vt_ref)

    dim = lax.broadcasted_iota(jnp.int32, (LANES, 1), 0)
    for p in range(n_pair):
        q_t = q_ref[0, :, pair_dims[p]].astype(F32).T
        for n in range(4):
            keep = (dim >= n * DIFF_HALF) & (dim < (n + 1) * DIFF_HALF)
            q4t_ref[:, p * width + n * tq:p * width + (n + 1) * tq] = jnp.where(keep, q_t, 0.0).astype(BF16)
    m_ref[...] = jnp.full(m_ref.shape, MASKED, F32)
    l_ref[...] = jnp.zeros(l_ref.shape, F32)
    acc_ref[...] = jnp.zeros(acc_ref.shape, F32)

    def scores(j, s_ref, masked):
        rows = pl.ds(pl.multiple_of(j * tk, tk), tk)
        if masked:
            key_pos = lax.broadcasted_iota(jnp.int32, (tk, width), 0) + j * tk
            query_pos = (lax.broadcasted_iota(jnp.int32, (tk, width), 1) & (tq - 1)) + i * tq
            causal = key_pos <= query_pos
        for p in range(n_pair):
            s_t = _dot(k_ref[0, rows, pair_dims[p]], q4t_ref[:, pair_cols[p]])
            _store_scores(jnp.where(causal, s_t, -jnp.inf) if masked else s_t, s_ref, pair_cols[p])

    def update(j, s_ref):
        _softmax_update_t(s_ref, [vt_ref[h, j] for h in range(2 * n_pair)], m_ref, l_ref, acc_ref)

    s0_ref, s1_ref = (s0_ref, smax0_ref), (s1_ref, smax1_ref)

    n_full = (i * tq) // tk
    scores(n_full, s0_ref, True)

    def chunk_pair(u, carry):
        scores(2 * u, s1_ref, False)
        update(jnp.where(u == 0, n_full, 2 * u - 1), s0_ref)
        scores(2 * u + 1, s0_ref, False)
        update(2 * u, s1_ref)
        return carry

    n_pairs = n_full // 2
    lax.fori_loop(0, n_pairs, chunk_pair, 0)
    pending = jnp.where(n_pairs == 0, n_full, 2 * n_pairs - 1)

    @pl.when(n_full % 2 == 1)
    def _():
        scores(n_full - 1, s1_ref, False)
        update(pending, s0_ref)
        update(n_full - 1, s1_ref)

    @pl.when(n_full % 2 == 0)
    def _():
        update(pending, s0_ref)

    lam = (jnp.exp(jnp.sum(lq1_ref[...] * lk1_ref[...], axis=-1, keepdims=True))
           - jnp.exp(jnp.sum(lq2_ref[...] * lk2_ref[...], axis=-1, keepdims=True)) + lam_init)
    for p in range(n_pair):
        o4 = acc_ref[:, pair_cols[p]] / l_ref[:, pair_cols[p]]
        heads = []
        for h in range(2):
            o = o4[:, 2 * h * tq:(2 * h + 1) * tq] - lam * o4[:, (2 * h + 1) * tq:(2 * h + 2) * tq]
            heads.append(o * lax.rsqrt(jnp.mean(o * o, axis=0, keepdims=True) + EPS))
        o = jnp.concatenate(heads, axis=0) * subln_ref[...] * (1.0 - lam_init)
        o_ref[0, :, pair_dims[p]] = o.T.astype(o_ref.dtype)


def _diff_attention(qkv, lam_vecs, subln, lam_init, tq=256, tk=512):
    batch, seq, _ = qkv.shape
    n_pair = W_DIFF // LANES
    vec = _resident((1, DIFF_HALF))
    kernel = functools.partial(_diff_kernel, tq=tq, tk=tk, lam_init=lam_init)
    lanes = n_pair * 4 * tq
    return pl.pallas_call(
        kernel,
        grid=(batch, seq // tq),
        in_specs=[vec, vec, vec, vec, _resident((LANES, 1)),
                  pl.BlockSpec((1, tq, W_DIFF), lambda b, i: (b, i, 0)),
                  pl.BlockSpec((1, seq, W_DIFF), lambda b, i: (b, 0, 1)),
                  pl.BlockSpec((1, seq, W_DIFF), lambda b, i: (b, 0, 2))],
        out_specs=pl.BlockSpec((1, tq, W_DIFF), lambda b, i: (b, i, 0)),
        out_shape=jax.ShapeDtypeStruct((batch, seq, W_DIFF), BF16),
        scratch_shapes=[pltpu.VMEM((2 * n_pair, seq // tk, HEAD_DIM + ONES_ROWS, tk), BF16),
                        pltpu.VMEM((LANES, lanes), BF16),
                        pltpu.VMEM((tk, lanes), F32), pltpu.VMEM((tk, lanes), F32),
                        pltpu.VMEM((1, lanes), F32), pltpu.VMEM((1, lanes), F32),
                        pltpu.VMEM((1, lanes), F32), pltpu.VMEM((1, lanes), F32),
                        pltpu.VMEM((HEAD_DIM, lanes), F32)],
        compiler_params=_params(2),
        name="diff_attention",
    )(*[v.reshape(1, DIFF_HALF) for v in lam_vecs], jnp.tile(subln, 2).reshape(LANES, 1), qkv, qkv, qkv)


def _dil_kernel(q_ref, k_ref, v_ref, o_ref, o0_ref, o1_ref, o2_ref, l0_ref, l1_ref, l2_ref, *, tile, group):
    base = pl.program_id(2) * tile
    w = DIL_STEPS
    lane = _lane_iota()
    qi = lax.broadcasted_iota(jnp.int32, (2 * w, 2 * w), 0) & (w - 1)
    kj = lax.broadcasted_iota(jnp.int32, (2 * w, 2 * w), 1)
    band = (kj >= qi) & (kj <= qi + w)

    for (window, d), ob_ref, lse_ref in zip(DILATED_PAIRS, (o0_ref, o1_ref, o2_ref), (l0_ref, l1_ref, l2_ref)):
        def rows(ref, start, d=d):
            if d == 1:
                return ref[0, pl.ds(start, w), :]
            return ref[0, pl.ds(start, w, stride=d), :]

        def block_group(grp, carry, window=window, d=d, ob_ref=ob_ref, lse_ref=lse_ref, rows=rows):
            local, prev, own, lo, s, e, m, o = [], [], [], [], [], [], [], []
            for g in range(group):
                blk = grp * group + g
                local.append((blk // d) * window + (blk & (d - 1)))
                own.append(base + local[g])
                has_prev = own[g] >= window
                prev.append(jnp.where(has_prev, own[g] - window, own[g]))
                lo.append(jnp.where(has_prev, 0, w))
                q = rows(q_ref, local[g])
                q2 = jnp.concatenate([jnp.where(lane < HEAD_DIM, q, 0.0), jnp.where(lane < HEAD_DIM, 0.0, q)], axis=0)
                k2 = jnp.concatenate([rows(k_ref, prev[g]), rows(k_ref, own[g])], axis=0)
                s.append(_dot_nt(q2.astype(BF16), k2.astype(BF16)))
            for g in range(group):
                sg = jnp.where(band & (kj >= lo[g]), s[g], -jnp.inf)
                m.append(jnp.max(sg, axis=-1, keepdims=True))
                e.append(jnp.exp2(sg - m[g]).astype(BF16))
            for g in range(group):
                v2 = jnp.concatenate([rows(v_ref, prev[g]), rows(v_ref, own[g])], axis=0).astype(BF16)
                o.append(_dot(e[g], jnp.concatenate([v2, jnp.ones_like(v2)], axis=1)))
            for g in range(group):
                first = lane < HEAD_DIM
                l = jnp.where(first, o[g][0:w, LANES:2 * LANES], o[g][w:2 * w, LANES:2 * LANES])
                ob = jnp.where(first, o[g][0:w, 0:LANES], o[g][w:2 * w, 0:LANES]) / l
                lb = jnp.where(first, m[g][0:w], m[g][w:2 * w]) + jnp.log2(l)
                if d == 1:
                    ob_ref[pl.ds(local[g], w), :] = ob
                    lse_ref[pl.ds(local[g], w), :] = lb
                else:
                    ob_ref[pl.ds(local[g], w, stride=d), :] = ob
                    lse_ref[pl.ds(local[g], w, stride=d), :] = lb
            return carry

        lax.fori_loop(0, tile // w // group, block_group, 0)

    lse = [l0_ref[...], l1_ref[...], l2_ref[...]]
    m = jnp.maximum(jnp.maximum(lse[0], lse[1]), lse[2])
    wts = [jnp.exp2(x - m) for x in lse]
    mixed = wts[0] * o0_ref[...] + wts[1] * o1_ref[...] + wts[2] * o2_ref[...]
    o_ref[0] = (mixed / (wts[0] + wts[1] + wts[2])).astype(o_ref.dtype)


def _dilated_attention(qkv, group=8):
    batch, seq, _ = qkv.shape
    n_pairs = W_DIL // LANES
    tile = max(window for window, _ in DILATED_PAIRS)
    assert seq % tile == 0
    return pl.pallas_call(
        functools.partial(_dil_kernel, tile=tile, group=group),
        grid=(batch, n_pairs, seq // tile),
        in_specs=[pl.BlockSpec((1, tile, LANES), lambda b, p, i: (b, i, p)),
                  pl.BlockSpec((1, seq, LANES), lambda b, p, i: (b, 0, n_pairs + p)),
                  pl.BlockSpec((1, seq, LANES), lambda b, p, i: (b, 0, 2 * n_pairs + p))],
        out_specs=pl.BlockSpec((1, tile, LANES), lambda b, p, i: (b, i, p)),
        out_shape=jax.ShapeDtypeStruct((batch, seq, W_DIL), BF16),
        scratch_shapes=[pltpu.VMEM((tile, LANES), F32)] * 6,
        compiler_params=_params(3),
        name="dilated_attention",
    )(qkv, qkv, qkv)


def _moba_kernel(q_ref, k_ref, v_ref, o_ref, kaug_ref, vt_ref, kmh_ref, kml_ref, qaugt_ref, so_ref, s0_ref, s1_ref,
                 smaxo_ref, smax0_ref, smax1_ref, m_ref, l_ref, acc_ref):
    i = pl.program_id(1)
    blk = MOBA_BLOCK
    tk = 2 * blk
    blk_shift = blk.bit_length() - 1
    seq = k_ref.shape[1]
    n_blk = seq // blk
    n_pair = vt_ref.shape[0] // 2
    width = 2 * blk
    pair_cols = [slice(p * width, (p + 1) * width) for p in range(n_pair)]
    pair_dims = [slice(p * LANES, (p + 1) * LANES) for p in range(n_pair)]

    @pl.when(i == 0)
    def _():
        qaugt_ref[LANES:2 * LANES, :] = jnp.zeros((LANES, n_pair * width), BF16)
        _transpose_values(v_ref, vt_ref)
        piece = 1024
        for p in range(n_pair):
            def one_piece(c, total, p=p):
                rows = pl.ds(pl.multiple_of(c * piece, piece), piece)
                kc = k_ref[0, rows, pair_dims[p]]
                key_blk = (lax.broadcasted_iota(jnp.int32, (piece, LANES), 0) + c * piece) >> blk_shift
                kaug_ref[p, rows, 0:LANES] = kc
                kaug_ref[p, rows, LANES:2 * LANES] = (
                    lax.broadcasted_iota(jnp.int32, (piece, LANES), 1) == key_blk).astype(BF16)
                member = (lax.broadcasted_iota(jnp.int32, (LANES, piece), 0)
                          == (lax.broadcasted_iota(jnp.int32, (LANES, piece), 1) + c * piece) >> blk_shift)
                return total + _dot(member.astype(BF16), kc)

            total = lax.fori_loop(0, seq // piece, one_piece, jnp.zeros((LANES, LANES), F32))
            k_mean = total * (1.0 / blk)
            hi = k_mean.astype(BF16)
            kmh_ref[p] = hi
            kml_ref[p] = (k_mean - hi.astype(F32)).astype(BF16)

    dim = lax.broadcasted_iota(jnp.int32, (LANES, 1), 0)
    q2t, gates = [], []
    for p in range(n_pair):
        q_t = q_ref[0, :, pair_dims[p]].astype(F32).T
        q2t.append(jnp.concatenate([jnp.where(dim < HEAD_DIM, q_t, 0.0), jnp.where(dim < HEAD_DIM, 0.0, q_t)],
                                   axis=1).astype(BF16))
        gates.append(_dot(kmh_ref[p, 0:n_blk, :], q2t[p]) + _dot(kml_ref[p, 0:n_blk, :], q2t[p]))

    gate = jnp.concatenate(gates, axis=1)
    blk_id = lax.broadcasted_iota(jnp.int32, gate.shape, 0)
    blk_f = blk_id.astype(F32)
    g = jnp.where(blk_id < i, gate, -jnp.inf)
    selected = jnp.zeros(gate.shape, F32)
    for _ in range(MOBA_TOPK):
        best = jnp.max(g, axis=0, keepdims=True)
        first = jnp.min(jnp.where(g == best, blk_f, float(LANES)), axis=0, keepdims=True)
        pick = (blk_f == first) & (best > -jnp.inf)
        selected = jnp.where(pick, 1.0, selected)
        g = jnp.where(pick, -jnp.inf, g)
    for p in range(n_pair):
        qaugt_ref[0:LANES, pair_cols[p]] = q2t[p]
    qaugt_ref[LANES:LANES + n_blk, :] = jnp.where(selected > 0.0, 0.0, MASKED).astype(BF16)

    m_ref[...] = jnp.full(m_ref.shape, MASKED, F32)
    l_ref[...] = jnp.zeros(l_ref.shape, F32)
    acc_ref[...] = jnp.zeros(acc_ref.shape, F32)

    def scores(c, s_ref):
        rows = pl.ds(pl.multiple_of(c * tk, tk), tk)
        for p in range(n_pair):
            _store_scores(_dot(kaug_ref[p, rows, :], qaugt_ref[:, pair_cols[p]]), s_ref, pair_cols[p])

    def update(c, s_ref):
        v_ts = [jnp.concatenate([vt_ref[h, 2 * c], vt_ref[h, 2 * c + 1]], axis=1) for h in range(2 * n_pair)]
        _softmax_update_t(s_ref, v_ts, m_ref, l_ref, acc_ref)

    so_ref, s0_ref, s1_ref = (so_ref, smaxo_ref), (s0_ref, smax0_ref), (s1_ref, smax1_ref)

    own_rows = pl.ds(pl.multiple_of(i * blk, blk), blk)
    key = lax.broadcasted_iota(jnp.int32, (blk, width), 0)
    query = lax.broadcasted_iota(jnp.int32, (blk, width), 1) & (blk - 1)
    for p in range(n_pair):
        s_own = _dot(kaug_ref[p, own_rows, 0:LANES], q2t[p])
        _store_scores(jnp.where(key <= query, s_own, -jnp.inf), so_ref, pair_cols[p])
    scores(0, s0_ref)
    _softmax_update_t(so_ref, [vt_ref[h, i] for h in range(2 * n_pair)], m_ref, l_ref, acc_ref)

    n = jnp.maximum((i + 1) // 2, 1)

    def chunk_pair(u, carry):
        scores(2 * u + 1, s1_ref)
        update(2 * u, s0_ref)
        scores(2 * u + 2, s0_ref)
        update(2 * u + 1, s1_ref)
        return carry

    lax.fori_loop(0, (n - 1) // 2, chunk_pair, 0)

    @pl.when(n % 2 == 1)
    def _():
        update(n - 1, s0_ref)

    @pl.when(n % 2 == 0)
    def _():
        scores(n - 1, s1_ref)
        update(n - 2, s0_ref)
        update(n - 1, s1_ref)

    for p in range(n_pair):
        o = acc_ref[:, pair_cols[p]] / l_ref[:, pair_cols[p]]
        o_ref[0, :, pair_dims[p]] = jnp.concatenate([o[:, 0:blk], o[:, blk:2 * blk]], axis=0).T.astype(o_ref.dtype)


def _moba_attention(qkv):
    batch, seq, _ = qkv.shape
    n_pair = W_MOBA // LANES
    blk = MOBA_BLOCK
    lanes = n_pair * 2 * blk
    assert seq % 1024 == 0 and seq // blk <= LANES and seq // blk >= MOBA_TOPK
    assert (seq // blk) % 16 == 0
    whole = lambda j: pl.BlockSpec((1, seq, W_MOBA), lambda b, i: (b, 0, j), pipeline_mode=pl.Buffered(1))
    return pl.pallas_call(
        _moba_kernel,
        grid=(batch, seq // blk),
        in_specs=[pl.BlockSpec((1, blk, W_MOBA), lambda b, i: (b, i, 0)), whole(1), whole(2)],
        out_specs=pl.BlockSpec((1, blk, W_MOBA), lambda b, i: (b, i, 0)),
        out_shape=jax.ShapeDtypeStruct((batch, seq, W_MOBA), BF16),
        scratch_shapes=[pltpu.VMEM((n_pair, seq, 2 * LANES), BF16),
                        pltpu.VMEM((2 * n_pair, seq // blk, HEAD_DIM + ONES_ROWS, blk), BF16),
                        pltpu.VMEM((n_pair, LANES, LANES), BF16), pltpu.VMEM((n_pair, LANES, LANES), BF16),
                        pltpu.VMEM((2 * LANES, lanes), BF16), pltpu.VMEM((blk, lanes), F32),
                        pltpu.VMEM((2 * blk, lanes), F32), pltpu.VMEM((2 * blk, lanes), F32),
                        pltpu.VMEM((1, lanes), F32), pltpu.VMEM((1, lanes), F32), pltpu.VMEM((1, lanes), F32),
                        pltpu.VMEM((1, lanes), F32), pltpu.VMEM((1, lanes), F32),
                        pltpu.VMEM((HEAD_DIM, lanes), F32)],
        compiler_params=_params(2),
        name="moba_attention",
    )(qkv, qkv, qkv)


def _mem_kv_kernel(mem_ref, nw_ref, w_ref, g64_ref, gain_ref, k_ref, v_ref):
    h = _rms(mem_ref[0], nw_ref[...]).astype(BF16)
    kv = _dot(h, w_ref[...])
    k = kv[:, 0:MEM_WIDTH]
    ms = _dot((k * k).astype(BF16), g64_ref[...])
    k_ref[0] = (k * lax.rsqrt(ms + EPS) * gain_ref[...]).astype(BF16)
    v_ref[0] = kv[:, MEM_WIDTH:2 * MEM_WIDTH].astype(BF16)


def _mem_kv(mem, norm_w, w_mkv, layer, k_gain):
    batch, mem_len, _ = mem.shape
    out = jax.ShapeDtypeStruct((batch, mem_len, MEM_WIDTH), BF16)
    return pl.pallas_call(
        _mem_kv_kernel,
        grid=(batch,),
        in_specs=[pl.BlockSpec((1, mem_len, D_MODEL), lambda b: (b, 0, 0)), _resident((1, D_MODEL)),
                  _layer_of(w_mkv, layer), _resident((MXU_DIM, MXU_DIM)), _resident((1, MEM_WIDTH))],
        out_specs=[pl.BlockSpec((1, mem_len, MEM_WIDTH), lambda b: (b, 0, 0))] * 2,
        out_shape=[out, out],
        compiler_params=_params(1),
        name="mem_kv",
    )(mem, norm_w.reshape(1, D_MODEL), w_mkv, _group_mean_matrix(HEAD_DIM),
      jnp.tile(k_gain, N_HEADS_MEM).reshape(1, MEM_WIDTH))


def _post_kernel(x_ref, a_ref, b_ref, c_ref, wo_ref, ncross_ref, wmq_ref, g64_ref, qgain_ref, km_ref, vm_ref,
                 wmo_ref, nffn_ref, wgu_ref, wd_ref, o_ref, *, tm):
    mix = jnp.concatenate([a_ref[...], b_ref[...], c_ref[...]], axis=1)
    x = x_ref[...] + _dot(mix, wo_ref[...])

    q = _dot(_rms(x, ncross_ref[...]).astype(BF16), wmq_ref[...])
    ms = _dot((q * q).astype(BF16), g64_ref[...])
    qn = (q * lax.rsqrt(ms + EPS) * qgain_ref[...]).astype(BF16)
    head = _lane_iota(MEM_WIDTH) >> (HEAD_DIM.bit_length() - 1)
    q4 = jnp.concatenate([jnp.where(head == n, qn, jnp.zeros_like(qn)) for n in range(N_HEADS_MEM)], axis=0)
    s = _dot_nt(q4, km_ref[0])
    e = jnp.exp(s - jnp.max(s, axis=-1, keepdims=True))
    o4 = _dot(e.astype(BF16), vm_ref[0]) / jnp.sum(e, axis=-1, keepdims=True)
    o = jnp.where(head == 0, o4[0:tm], 0.0)
    for n in range(1, N_HEADS_MEM):
        o = o + jnp.where(head == n, o4[n * tm:(n + 1) * tm], 0.0)
    x = x + _dot(o.astype(BF16), wmo_ref[...])

    h = _rms(x, nffn_ref[...]).astype(BF16)
    y = jnp.zeros((tm, D_MODEL), F32)
    for c in range(D_FF // MXU_DIM):
        lo = c * MXU_DIM
        g = _dot(h, wgu_ref[:, lo:lo + MXU_DIM])
        u = _dot(h, wgu_ref[:, D_FF + lo:D_FF + lo + MXU_DIM])
        y = y + _dot((g * jax.nn.sigmoid(g) * u).astype(BF16), wd_ref[lo:lo + MXU_DIM, :])
    o_ref[...] = x + y


def _post_attention(x2d, outs, seq, layer, w_out, norm_cross, w_mq, q_gain, k_mem, v_mem, w_mo, norm_ffn,
                    w_gate_up, w_down, tm=512):
    rows = x2d.shape[0]
    mem_len = k_mem.shape[1]
    tiles_per_seq = seq // tm
    row_spec = lambda width: pl.BlockSpec((tm, width), lambda i: (i, 0))
    mem_spec = pl.BlockSpec((1, mem_len, MEM_WIDTH), lambda i: (i // tiles_per_seq, 0, 0))
    scale = HEAD_DIM ** -0.5
    return pl.pallas_call(
        functools.partial(_post_kernel, tm=tm),
        grid=(rows // tm,),
        in_specs=[row_spec(D_MODEL), row_spec(W_DIFF), row_spec(W_DIL), row_spec(W_MOBA),
                  _layer_of(w_out, layer), _resident((1, D_MODEL)), _layer_of(w_mq, layer),
                  _resident((MXU_DIM, MXU_DIM)), _resident((1, MEM_WIDTH)), mem_spec, mem_spec,
                  _layer_of(w_mo, layer), _resident((1, D_MODEL)), _layer_of(w_gate_up, layer),
                  _layer_of(w_down, layer)],
        out_specs=row_spec(D_MODEL),
        out_shape=jax.ShapeDtypeStruct((rows, D_MODEL), F32),
        compiler_params=_params(1),
        name="post_attention",
    )(x2d, *[o.reshape(rows, -1) for o in outs], w_out, norm_cross.reshape(1, D_MODEL),
      w_mq, _group_mean_matrix(HEAD_DIM), (jnp.tile(q_gain, N_HEADS_MEM) * scale).reshape(1, MEM_WIDTH),
      k_mem, v_mem, w_mo, norm_ffn.reshape(1, D_MODEL), w_gate_up, w_down)


def kernel(x, mem, positions, norm_mix, w_in, qn_diff, kn_diff, lambda_q1, lambda_k1, lambda_q2, lambda_k2,
           subln_diff, qn_dil, kn_dil, qn_moba, kn_moba, w_out, norm_cross, norm_mem, w_mq, w_mkv, qn_mem,
           kn_mem, w_mo, norm_ffn, w_gate_up, w_down):
    batch, seq, _ = x.shape
    depth = w_in.shape[0]
    rows = batch * seq
    tables = _rope_tables(positions)
    x2d = x.reshape(rows, D_MODEL)
    w_in, w_out, w_mq, w_mkv, w_mo, w_gate_up, w_down = (
        w.astype(BF16) for w in (w_in, w_out, w_mq, w_mkv, w_mo, w_gate_up, w_down))
    for l in range(depth):
        gains = jnp.concatenate([
            jnp.tile(qn_diff[l], 2 * N_HEADS_DIFF) * (DIFF_HALF ** -0.5 * LOG2_E),
            jnp.tile(qn_dil[l], N_HEADS_DIL) * (HEAD_DIM ** -0.5 * LOG2_E),
            jnp.tile(qn_moba[l], N_HEADS_MOBA) * (HEAD_DIM ** -0.5 * LOG2_E),
            jnp.tile(kn_diff[l], 2 * N_HEADS_DIFF), jnp.tile(kn_dil[l], N_HEADS_DIL),
            jnp.tile(kn_moba[l], N_HEADS_MOBA)]).reshape(1, 2 * MIX_WIDTH)
        qkv_a, qkv_b, qkv_c = _qkv_project(x2d, norm_mix[l], w_in, l, gains, tables)
        lam_init = 0.8 - 0.6 * math.exp(-0.3 * l)
        o_a = _diff_attention(qkv_a.reshape(batch, seq, -1),
                              (lambda_q1[l], lambda_k1[l], lambda_q2[l], lambda_k2[l]), subln_diff[l], lam_init)
        o_b = _dilated_attention(qkv_b.reshape(batch, seq, -1))
        o_c = _moba_attention(qkv_c.reshape(batch, seq, -1))
        k_mem, v_mem = _mem_kv(mem, norm_mem[l], w_mkv, l, kn_mem[l])
        x2d = _post_attention(x2d, (o_a, o_b, o_c), seq, l, w_out, norm_cross[l], w_mq, qn_mem[l], k_mem, v_mem,
                              w_mo, norm_ffn[l], w_gate_up, w_down)
    return x2d.reshape(batch, seq, D_MODEL)
```

```python
import functools
import math

import jax
import jax.numpy as jnp
import numpy as np
from jax import lax
from jax.experimental import pallas as pl
from jax.experimental.pallas import tpu as pltpu

D_MODEL = 1024
HEAD_DIM = 64
N_HEADS_DIFF = 4
N_HEADS_DIL = 6
N_HEADS_MOBA = 6
DIFF_HALF = HEAD_DIM // 2
ROPE_THETA = 500000.0
ROPE_FRACTION = 4
DILATED_PAIRS = ((128, 1), (512, 4), (2048, 16))
DIL_STEPS = 128
MOBA_BLOCK = 256
MOBA_TOPK = 3
MEM_LEN = 256
N_HEADS_MEM = 4
MEM_WIDTH = N_HEADS_MEM * HEAD_DIM
D_FF = 2816
EPS = 1e-6

W_DIFF = N_HEADS_DIFF * HEAD_DIM
W_DIL = N_HEADS_DIL * HEAD_DIM
W_MOBA = N_HEADS_MOBA * HEAD_DIM
MIX_WIDTH = W_DIFF + W_DIL + W_MOBA

LANES = 128
MXU_DIM = 256
VMEM_LIMIT_BYTES = 56 * 2**20

MASKED = -1e30
LOG2_E = math.log2(math.e)

BF16 = jnp.bfloat16
F32 = jnp.float32
NT_DIMS = (((1,), (1,)), ((), ()))


def _dot(a, b):
    return jnp.dot(a, b, preferred_element_type=F32)


def _dot_nt(a, b):
    return lax.dot_general(a, b, NT_DIMS, preferred_element_type=F32)


def _params(n_grid):
    return pltpu.CompilerParams(dimension_semantics=("arbitrary",) * n_grid,
                                vmem_limit_bytes=VMEM_LIMIT_BYTES)


def _resident(shape):
    nd = len(shape)
    return pl.BlockSpec(shape, lambda *_: (0,) * nd, pipeline_mode=pl.Buffered(1))


def _layer_of(stacked, layer):
    tail = stacked.shape[1:]
    return pl.BlockSpec((None,) + tail, lambda *_: (layer,) + (0,) * len(tail), pipeline_mode=pl.Buffered(1))


def _rms(x, w):
    return x * lax.rsqrt(jnp.mean(x * x, axis=-1, keepdims=True) + EPS) * w


def _lane_iota(n=LANES):
    return lax.broadcasted_iota(jnp.int32, (1, n), 1)


N_FREQ = 16


def _rope_kernel(pos_ref, freq_ref, sel_ref, ch_ref, cd_ref, sh_ref, sd_ref):
    tm = pos_ref.shape[1]
    ang = freq_ref[...] * pos_ref[...].astype(F32)
    pad = jnp.zeros((LANES - N_FREQ, tm), F32)
    for trig, tables in ((jnp.cos, ((0, ch_ref), (1, cd_ref))), (jnp.sin, ((2, sh_ref), (3, sd_ref)))):
        x = jnp.concatenate([trig(ang), pad], axis=0).T
        hi = x.astype(BF16)
        rest = x - hi.astype(F32)
        mid = rest.astype(BF16)
        lo = (rest - mid.astype(F32)).astype(BF16)
        for k, out_ref in tables:
            out_ref[...] = _dot(hi, sel_ref[k]) + _dot(mid, sel_ref[k]) + _dot(lo, sel_ref[k])


def _rope_selectors():
    sel = np.zeros((4, LANES, LANES), np.float32)
    one_row = N_FREQ - 1
    row0 = 0
    for kind, period in ((0, HEAD_DIM), (1, DIFF_HALF)):
        half = period // ROPE_FRACTION // 2
        for lane in range(LANES):
            within = lane % period
            if within < 2 * half:
                sel[kind, row0 + within % half, lane] = 1.0
                sel[2 + kind, row0 + within % half, lane] = -1.0 if within < half else 1.0
            else:
                sel[kind, one_row, lane] = 1.0
        row0 += half
    return jnp.asarray(sel, BF16)


def _rope_tables(positions):
    rows = positions.size
    tm = 2048
    freqs = [ROPE_THETA ** (-jnp.arange(half, dtype=F32) / half)
             for half in (HEAD_DIM // ROPE_FRACTION // 2, DIFF_HALF // ROPE_FRACTION // 2)]
    freqs = jnp.concatenate(freqs + [jnp.zeros((N_FREQ - sum(f.size for f in freqs),), F32)]).reshape(N_FREQ, 1)
    table = jax.ShapeDtypeStruct((rows, LANES), F32)
    ch, cd, sh, sd = pl.pallas_call(
        _rope_kernel,
        grid=(rows // tm,),
        in_specs=[pl.BlockSpec((1, tm), lambda i: (0, i)), _resident((N_FREQ, 1)), _resident((4, LANES, LANES))],
        out_specs=[pl.BlockSpec((tm, LANES), lambda i: (i, 0))] * 4,
        out_shape=[table] * 4,
        compiler_params=_params(1),
        name="rope_tables",
    )(positions.reshape(1, rows), freqs, _rope_selectors())
    return ch, sh, cd, sd


def _group_mean_matrix(group):
    idx = jnp.arange(MXU_DIM) // group
    return jnp.where(idx[:, None] == idx[None, :], 1.0 / group, 0.0).astype(BF16)


def _rotate(t, cos, sin, half, period):
    first = (_lane_iota() & (period - 1)) < half
    partner = jnp.where(first, pltpu.roll(t, LANES - half, 1), pltpu.roll(t, half, 1))
    return t * cos + partner * sin


def _qkv_kernel(x_ref, nw_ref, w_ref, gain_ref, g32_ref, g64_ref, ch_ref, sh_ref, cd_ref, sd_ref,
                a_ref, b_ref, c_ref):
    h = _rms(x_ref[...], nw_ref[...]).astype(BF16)
    wide = 2 * MXU_DIM
    for chunk in range(3 * MIX_WIDTH // MXU_DIM):
        section, col0 = divmod(chunk * MXU_DIM, MIX_WIDTH)
        if chunk % 2 == 0:
            y_wide = _dot(h, w_ref[:, chunk * MXU_DIM:chunk * MXU_DIM + wide])
        y = y_wide[:, (chunk % 2) * MXU_DIM:(chunk % 2 + 1) * MXU_DIM]
        if section < 2:
            is_diff = col0 < W_DIFF
            g_ref = g32_ref if is_diff else g64_ref
            ms = _dot((y * y).astype(BF16), g_ref[...])
            y = y * lax.rsqrt(ms + EPS) * gain_ref[:, chunk * MXU_DIM:(chunk + 1) * MXU_DIM]
        for piece in range(MXU_DIM // LANES):
            col = col0 + piece * LANES
            t = y[:, piece * LANES:(piece + 1) * LANES]
            if section < 2:
                if col < W_DIFF:
                    t = _rotate(t, cd_ref[...], sd_ref[...], DIFF_HALF // ROPE_FRACTION // 2, DIFF_HALF)
                else:
                    t = _rotate(t, ch_ref[...], sh_ref[...], HEAD_DIM // ROPE_FRACTION // 2, HEAD_DIM)
            if col < W_DIFF:
                dst = section * W_DIFF + col
                a_ref[:, dst:dst + LANES] = t.astype(BF16)
            elif col < W_DIFF + W_DIL:
                dst = section * W_DIL + col - W_DIFF
                b_ref[:, dst:dst + LANES] = t
            else:
                dst = section * W_MOBA + col - W_DIFF - W_DIL
                c_ref[:, dst:dst + LANES] = t.astype(BF16)


def _qkv_project(x2d, norm_w, w_in, layer, gains, tables, tm=512):
    rows = x2d.shape[0]
    row_spec = lambda width: pl.BlockSpec((tm, width), lambda i: (i, 0))
    return pl.pallas_call(
        _qkv_kernel,
        grid=(rows // tm,),
        in_specs=[row_spec(D_MODEL), _resident((1, D_MODEL)), _layer_of(w_in, layer),
                  _resident((1, 2 * MIX_WIDTH)), _resident((MXU_DIM, MXU_DIM)), _resident((MXU_DIM, MXU_DIM))]
                 + [row_spec(LANES)] * 4,
        out_specs=[row_spec(3 * W_DIFF), row_spec(3 * W_DIL), row_spec(3 * W_MOBA)],
        out_shape=[jax.ShapeDtypeStruct((rows, 3 * W_DIFF), BF16),
                   jax.ShapeDtypeStruct((rows, 3 * W_DIL), F32),
                   jax.ShapeDtypeStruct((rows, 3 * W_MOBA), BF16)],
        compiler_params=_params(1),
        name="qkv_project",
    )(x2d, norm_w.reshape(1, D_MODEL), w_in, gains,
      _group_mean_matrix(DIFF_HALF), _group_mean_matrix(HEAD_DIM), *tables)


def _reduce_rows(x, pair_op, final_op):
    n = x.shape[0]
    while n > 8 and n % 2 == 0:
        n //= 2
        x = pair_op(x[:n], x[n:])
    return final_op(x, axis=0, keepdims=True)


ONES_ROWS = 16


def _store_scores(s_t, buf, cols):
    s_ref, smax_ref = buf
    s_ref[:, cols] = s_t
    smax_ref[:, cols] = _reduce_rows(s_t, jnp.maximum, jnp.max)


def _softmax_update_t(buf, v_ts, m_ref, l_ref, acc_ref):
    s_ref, smax_ref = buf
    width = s_ref.shape[1] // len(v_ts)
    m_old = m_ref[...]
    m_new = jnp.maximum(m_old, smax_ref[...])
    alpha = jnp.exp2(m_old - m_new)
    e = jnp.exp2(s_ref[...] - m_new).astype(BF16)
    for h, v_t in enumerate(v_ts):
        cols = slice(h * width, (h + 1) * width)
        pv = _dot(v_t, e[:, cols])
        acc_ref[:, cols] = alpha[:, cols] * acc_ref[:, cols] + pv[0:HEAD_DIM]
        l_ref[:, cols] = alpha[:, cols] * l_ref[:, cols] + pv[HEAD_DIM:HEAD_DIM + 1]
    m_ref[...] = m_new


def _transpose_values(v_ref, vt_ref):
    n_head, n_chunk, _, tk = vt_ref.shape

    def one_chunk(c, carry):
        rows = pl.ds(pl.multiple_of(c * tk, tk), tk)
        for p in range(n_head // 2):
            pair_t = v_ref[0, rows, p * LANES:(p + 1) * LANES].astype(F32).T.astype(BF16)
            for h in range(2):
                vt_ref[2 * p + h, c, 0:HEAD_DIM, :] = pair_t[h * HEAD_DIM:(h + 1) * HEAD_DIM]
                vt_ref[2 * p + h, c, HEAD_DIM:HEAD_DIM + ONES_ROWS, :] = jnp.ones((ONES_ROWS, tk), BF16)
        return carry

    lax.fori_loop(0, n_chunk, one_chunk, 0)


def _diff_kernel(lq1_ref, lk1_ref, lq2_ref, lk2_ref, subln_ref, q_ref, k_ref, v_ref, o_ref,
                 vt_ref, q4t_ref, s0_ref, s1_ref, smax0_ref, smax1_ref, m_ref, l_ref, acc_ref, *, tq, tk, lam_init):
    i = pl.program_id(1)
    n_pair = vt_ref.shape[0] // 2
    width = 4 * tq
    pair_cols = [slice(p * width, (p + 1) * width) for p in range(n_pair)]
    pair_dims = [slice(p * LANES, (p + 1) * LANES) for p in range(n_pair)]

    @pl.when(i == 0)
    def _():
        _transpose_values(v_ref, vt_ref)

    dim = lax.broadcasted_iota(jnp.int32, (LANES, 1), 0)
    for p in range(n_pair):
        q_t = q_ref[0, :, pair_dims[p]].astype(F32).T
        for n in range(4):
            keep = (dim >= n * DIFF_HALF) & (dim < (n + 1) * DIFF_HALF)
            q4t_ref[:, p * width + n * tq:p * width + (n + 1) * tq] = jnp.where(keep, q_t, 0.0).astype(BF16)
    m_ref[...] = jnp.full(m_ref.shape, MASKED, F32)
    l_ref[...] = jnp.zeros(l_ref.shape, F32)
    acc_ref[...] = jnp.zeros(acc_ref.shape, F32)

    def scores(j, s_ref, masked):
        rows = pl.ds(pl.multiple_of(j * tk, tk), tk)
        if masked:
            key_pos = lax.broadcasted_iota(jnp.int32, (tk, width), 0) + j * tk
            query_pos = (lax.broadcasted_iota(jnp.int32, (tk, width), 1) & (tq - 1)) + i * tq
            causal = key_pos <= query_pos
        for p in range(n_pair):
            s_t = _dot(k_ref[0, rows, pair_dims[p]], q4t_ref[:, pair_cols[p]])
            _store_scores(jnp.where(causal, s_t, -jnp.inf) if masked else s_t, s_ref, pair_cols[p])

    def update(j, s_ref):
        _softmax_update_t(s_ref, [vt_ref[h, j] for h in range(2 * n_pair)], m_ref, l_ref, acc_ref)

    s0_ref, s1_ref = (s0_ref, smax0_ref), (s1_ref, smax1_ref)

    n_full = (i * tq) // tk
    scores(n_full, s0_ref, True)

    def chunk_pair(u, carry):
        scores(2 * u, s1_ref, False)
        update(jnp.where(u == 0, n_full, 2 * u - 1), s0_ref)
        scores(2 * u + 1, s0_ref, False)
        update(2 * u, s1_ref)
        return carry

    n_pairs = n_full // 2
    lax.fori_loop(0, n_pairs, chunk_pair, 0)
    pending = jnp.where(n_pairs == 0, n_full, 2 * n_pairs - 1)

    @pl.when(n_full % 2 == 1)
    def _():
        scores(n_full - 1, s1_ref, False)
        update(pending, s0_ref)
        update(n_full - 1, s1_ref)

    @pl.when(n_full % 2 == 0)
    def _():
        update(pending, s0_ref)

    lam = (jnp.exp(jnp.sum(lq1_ref[...] * lk1_ref[...], axis=-1, keepdims=True))
           - jnp.exp(jnp.sum(lq2_ref[...] * lk2_ref[...], axis=-1, keepdims=True)) + lam_init)
    for p in range(n_pair):
        o4 = acc_ref[:, pair_cols[p]] / l_ref[:, pair_cols[p]]
        heads = []
        for h in range(2):
            o = o4[:, 2 * h * tq:(2 * h + 1) * tq] - lam * o4[:, (2 * h + 1) * tq:(2 * h + 2) * tq]
            heads.append(o * lax.rsqrt(jnp.mean(o * o, axis=0, keepdims=True) + EPS))
        o = jnp.concatenate(heads, axis=0) * subln_ref[...] * (1.0 - lam_init)
        o_ref[0, :, pair_dims[p]] = o.T.astype(o_ref.dtype)


def _diff_attention(qkv, lam_vecs, subln, lam_init, tq=256, tk=512):
    batch, seq, _ = qkv.shape
    n_pair = W_DIFF // LANES
    vec = _resident((1, DIFF_HALF))
    kernel = functools.partial(_diff_kernel, tq=tq, tk=tk, lam_init=lam_init)
    lanes = n_pair * 4 * tq
    return pl.pallas_call(
        kernel,
        grid=(batch, seq // tq),
        in_specs=[vec, vec, vec, vec, _resident((LANES, 1)),
                  pl.BlockSpec((1, tq, W_DIFF), lambda b, i: (b, i, 0)),
                  pl.BlockSpec((1, seq, W_DIFF), lambda b, i: (b, 0, 1)),
                  pl.BlockSpec((1, seq, W_DIFF), lambda b, i: (b, 0, 2))],
        out_specs=pl.BlockSpec((1, tq, W_DIFF), lambda b, i: (b, i, 0)),
        out_shape=jax.ShapeDtypeStruct((batch, seq, W_DIFF), BF16),
        scratch_shapes=[pltpu.VMEM((2 * n_pair, seq // tk, HEAD_DIM + ONES_ROWS, tk), BF16),
                        pltpu.VMEM((LANES, lanes), BF16),
                        pltpu.VMEM((tk, lanes), F32), pltpu.VMEM((tk, lanes), F32),
                        pltpu.VMEM((1, lanes), F32), pltpu.VMEM((1, lanes), F32),
                        pltpu.VMEM((1, lanes), F32), pltpu.VMEM((1, lanes), F32),
                        pltpu.VMEM((HEAD_DIM, lanes), F32)],
        compiler_params=_params(2),
        name="diff_attention",
    )(*[v.reshape(1, DIFF_HALF) for v in lam_vecs], jnp.tile(subln, 2).reshape(LANES, 1), qkv, qkv, qkv)


def _dil_kernel(q_ref, k_ref, v_ref, o_ref, o0_ref, o1_ref, o2_ref, l0_ref, l1_ref, l2_ref, *, tile, group):
    base = pl.program_id(2) * tile
    w = DIL_STEPS
    lane = _lane_iota()
    qi = lax.broadcasted_iota(jnp.int32, (2 * w, 2 * w), 0) & (w - 1)
    kj = lax.broadcasted_iota(jnp.int32, (2 * w, 2 * w), 1)
    band_bias = jnp.where((kj >= qi) & (kj <= qi + w), 0.0, -jnp.inf)

    for (window, d), ob_ref, lse_ref in zip(DILATED_PAIRS, (o0_ref, o1_ref, o2_ref), (l0_ref, l1_ref, l2_ref)):
        def rows(ref, start, d=d):
            if d == 1:
                return ref[0, pl.ds(start, w), :]
            return ref[0, pl.ds(start, w, stride=d), :]

        def block_group(grp, carry, window=window, d=d, ob_ref=ob_ref, lse_ref=lse_ref, rows=rows):
            local, prev, own, no_prev_bias, s, e, m, o = {}, {}, {}, {}, {}, {}, {}, {}

            def stage_scores(g):
                blk = grp * group + g
                local[g] = (blk // d) * window + (blk & (d - 1))
                own[g] = base + local[g]
                has_prev = own[g] >= window
                prev[g] = jnp.where(has_prev, own[g] - window, own[g])
                no_prev_bias[g] = jnp.where(has_prev, 0.0, -jnp.inf)
                q = rows(q_ref, local[g])
                q2 = jnp.concatenate([jnp.where(lane < HEAD_DIM, q, 0.0), jnp.where(lane < HEAD_DIM, 0.0, q)], axis=0)
                k2 = jnp.concatenate([rows(k_ref, prev[g]), rows(k_ref, own[g])], axis=0)
                s[g] = _dot_nt(q2.astype(BF16), k2.astype(BF16))

            def stage_softmax(g):
                sg = s.pop(g) + band_bias
                sg = jnp.concatenate([sg[:, 0:w] + no_prev_bias[g], sg[:, w:2 * w]], axis=1)
                m[g] = jnp.max(sg, axis=-1, keepdims=True)
                e[g] = jnp.exp2(sg - m[g]).astype(BF16)

            def stage_values(g):
                v2 = jnp.concatenate([rows(v_ref, prev[g]), rows(v_ref, own[g])], axis=0).astype(BF16)
                o[g] = _dot(e.pop(g), jnp.concatenate([v2, jnp.ones_like(v2)], axis=1))

            def stage_output(g):
                first = lane < HEAD_DIM
                og = o.pop(g)
                l = jnp.where(first, og[0:w, LANES:2 * LANES], og[w:2 * w, LANES:2 * LANES])
                ob = jnp.where(first, og[0:w, 0:LANES], og[w:2 * w, 0:LANES]) / l
                lb = jnp.where(first, m[g][0:w], m[g][w:2 * w]) + jnp.log2(l)
                if d == 1:
                    ob_ref[pl.ds(local[g], w), :] = ob
                    lse_ref[pl.ds(local[g], w), :] = lb
                else:
                    ob_ref[pl.ds(local[g], w, stride=d), :] = ob
                    lse_ref[pl.ds(local[g], w, stride=d), :] = lb

            stages = (stage_scores, stage_softmax, stage_values, stage_output)
            for step in range(group + len(stages) - 1):
                for lag, stage in enumerate(stages):
                    if 0 <= step - lag < group:
                        stage(step - lag)
            return carry

        lax.fori_loop(0, tile // w // group, block_group, 0)

    lse = [l0_ref[...], l1_ref[...], l2_ref[...]]
    m = jnp.maximum(jnp.maximum(lse[0], lse[1]), lse[2])
    wts = [jnp.exp2(x - m) for x in lse]
    mixed = wts[0] * o0_ref[...] + wts[1] * o1_ref[...] + wts[2] * o2_ref[...]
    o_ref[0] = (mixed / (wts[0] + wts[1] + wts[2])).astype(o_ref.dtype)


def _dilated_attention(qkv, group=16):
    batch, seq, _ = qkv.shape
    n_pairs = W_DIL // LANES
    tile = max(window for window, _ in DILATED_PAIRS)
    assert seq % tile == 0
    return pl.pallas_call(
        functools.partial(_dil_kernel, tile=tile, group=group),
        grid=(batch, n_pairs, seq // tile),
        in_specs=[pl.BlockSpec((1, tile, LANES), lambda b, p, i: (b, i, p)),
                  pl.BlockSpec((1, seq, LANES), lambda b, p, i: (b, 0, n_pairs + p)),
                  pl.BlockSpec((1, seq, LANES), lambda b, p, i: (b, 0, 2 * n_pairs + p))],
        out_specs=pl.BlockSpec((1, tile, LANES), lambda b, p, i: (b, i, p)),
        out_shape=jax.ShapeDtypeStruct((batch, seq, W_DIL), BF16),
        scratch_shapes=[pltpu.VMEM((tile, LANES), F32)] * 6,
        compiler_params=_params(3),
        name="dilated_attention",
    )(qkv, qkv, qkv)


def _moba_kernel(q_ref, k_ref, v_ref, o_ref, kaug_ref, vt_ref, kmh_ref, kml_ref, qaugt_ref, so_ref, s0_ref, s1_ref,
                 smaxo_ref, smax0_ref, smax1_ref, m_ref, l_ref, acc_ref):
    i = pl.program_id(1)
    blk = MOBA_BLOCK
    tk = 2 * blk
    blk_shift = blk.bit_length() - 1
    seq = k_ref.shape[1]
    n_blk = seq // blk
    n_pair = vt_ref.shape[0] // 2
    width = 2 * blk
    pair_cols = [slice(p * width, (p + 1) * width) for p in range(n_pair)]
    pair_dims = [slice(p * LANES, (p + 1) * LANES) for p in range(n_pair)]

    @pl.when(i == 0)
    def _():
        qaugt_ref[LANES:2 * LANES, :] = jnp.zeros((LANES, n_pair * width), BF16)
        _transpose_values(v_ref, vt_ref)
        piece = 1024
        for p in range(n_pair):
            def one_piece(c, total, p=p):
                rows = pl.ds(pl.multiple_of(c * piece, piece), piece)
                kc = k_ref[0, rows, pair_dims[p]]
                key_blk = (lax.broadcasted_iota(jnp.int32, (piece, LANES), 0) + c * piece) >> blk_shift
                kaug_ref[p, rows, 0:LANES] = kc
                kaug_ref[p, rows, LANES:2 * LANES] = (
                    lax.broadcasted_iota(jnp.int32, (piece, LANES), 1) == key_blk).astype(BF16)
                member = (lax.broadcasted_iota(jnp.int32, (LANES, piece), 0)
                          == (lax.broadcasted_iota(jnp.int32, (LANES, piece), 1) + c * piece) >> blk_shift)
                return total + _dot(member.astype(BF16), kc)

            total = lax.fori_loop(0, seq // piece, one_piece, jnp.zeros((LANES, LANES), F32))
            k_mean = total * (1.0 / blk)
            hi = k_mean.astype(BF16)
            kmh_ref[p] = hi
            kml_ref[p] = (k_mean - hi.astype(F32)).astype(BF16)

    dim = lax.broadcasted_iota(jnp.int32, (LANES, 1), 0)
    q2t, gates = [], []
    for p in range(n_pair):
        q_t = q_ref[0, :, pair_dims[p]].astype(F32).T
        q2t.append(jnp.concatenate([jnp.where(dim < HEAD_DIM, q_t, 0.0), jnp.where(dim < HEAD_DIM, 0.0, q_t)],
                                   axis=1).astype(BF16))
        gates.append(_dot(kmh_ref[p, 0:n_blk, :], q2t[p]) + _dot(kml_ref[p, 0:n_blk, :], q2t[p]))

    gate = jnp.concatenate(gates, axis=1)
    blk_id = lax.broadcasted_iota(jnp.int32, gate.shape, 0)
    blk_f = blk_id.astype(F32)
    g = jnp.where(blk_id < i, gate, -jnp.inf)
    selected = jnp.zeros(gate.shape, F32)
    for _ in range(MOBA_TOPK):
        best = jnp.max(g, axis=0, keepdims=True)
        first = jnp.min(jnp.where(g == best, blk_f, float(LANES)), axis=0, keepdims=True)
        pick = (blk_f == first) & (best > -jnp.inf)
        selected = jnp.where(pick, 1.0, selected)
        g = jnp.where(pick, -jnp.inf, g)
    for p in range(n_pair):
        qaugt_ref[0:LANES, pair_cols[p]] = q2t[p]
    qaugt_ref[LANES:LANES + n_blk, :] = jnp.where(selected > 0.0, 0.0, MASKED).astype(BF16)

    m_ref[...] = jnp.full(m_ref.shape, MASKED, F32)
    l_ref[...] = jnp.zeros(l_ref.shape, F32)
    acc_ref[...] = jnp.zeros(acc_ref.shape, F32)

    def scores(c, s_ref):
        rows = pl.ds(pl.multiple_of(c * tk, tk), tk)
        for p in range(n_pair):
            _store_scores(_dot(kaug_ref[p, rows, :], qaugt_ref[:, pair_cols[p]]), s_ref, pair_cols[p])

    def update(c, s_ref):
        v_ts = [jnp.concatenate([vt_ref[h, 2 * c], vt_ref[h, 2 * c + 1]], axis=1) for h in range(2 * n_pair)]
        _softmax_update_t(s_ref, v_ts, m_ref, l_ref, acc_ref)

    so_ref, s0_ref, s1_ref = (so_ref, smaxo_ref), (s0_ref, smax0_ref), (s1_ref, smax1_ref)

    own_rows = pl.ds(pl.multiple_of(i * blk, blk), blk)
    key = lax.broadcasted_iota(jnp.int32, (blk, width), 0)
    query = lax.broadcasted_iota(jnp.int32, (blk, width), 1) & (blk - 1)
    for p in range(n_pair):
        s_own = _dot(kaug_ref[p, own_rows, 0:LANES], q2t[p])
        _store_scores(jnp.where(key <= query, s_own, -jnp.inf), so_ref, pair_cols[p])
    scores(0, s0_ref)
    _softmax_update_t(so_ref, [vt_ref[h, i] for h in range(2 * n_pair)], m_ref, l_ref, acc_ref)

    n = jnp.maximum((i + 1) // 2, 1)

    def chunk_pair(u, carry):
        scores(2 * u + 1, s1_ref)
        update(2 * u, s0_ref)
        scores(2 * u + 2, s0_ref)
        update(2 * u + 1, s1_ref)
        return carry

    lax.fori_loop(0, (n - 1) // 2, chunk_pair, 0)

    @pl.when(n % 2 == 1)
    def _():
        update(n - 1, s0_ref)

    @pl.when(n % 2 == 0)
    def _():
        scores(n - 1, s1_ref)
        update(n - 2, s0_ref)
        update(n - 1, s1_ref)

    for p in range(n_pair):
        o = acc_ref[:, pair_cols[p]] / l_ref[:, pair_cols[p]]
        o_ref[0, :, pair_dims[p]] = jnp.concatenate([o[:, 0:blk], o[:, blk:2 * blk]], axis=0).T.astype(o_ref.dtype)


def _moba_attention(qkv):
    batch, seq, _ = qkv.shape
    n_pair = W_MOBA // LANES
    blk = MOBA_BLOCK
    lanes = n_pair * 2 * blk
    assert seq % 1024 == 0 and seq // blk <= LANES and seq // blk >= MOBA_TOPK
    assert (seq // blk) % 16 == 0
    whole = lambda j: pl.BlockSpec((1, seq, W_MOBA), lambda b, i: (b, 0, j), pipeline_mode=pl.Buffered(1))
    return pl.pallas_call(
        _moba_kernel,
        grid=(batch, seq // blk),
        in_specs=[pl.BlockSpec((1, blk, W_MOBA), lambda b, i: (b, i, 0)), whole(1), whole(2)],
        out_specs=pl.BlockSpec((1, blk, W_MOBA), lambda b, i: (b, i, 0)),
        out_shape=jax.ShapeDtypeStruct((batch, seq, W_MOBA), BF16),
        scratch_shapes=[pltpu.VMEM((n_pair, seq, 2 * LANES), BF16),
                        pltpu.VMEM((2 * n_pair, seq // blk, HEAD_DIM + ONES_ROWS, blk), BF16),
                        pltpu.VMEM((n_pair, LANES, LANES), BF16), pltpu.VMEM((n_pair, LANES, LANES), BF16),
                        pltpu.VMEM((2 * LANES, lanes), BF16), pltpu.VMEM((blk, lanes), F32),
                        pltpu.VMEM((2 * blk, lanes), F32), pltpu.VMEM((2 * blk, lanes), F32),
                        pltpu.VMEM((1, lanes), F32), pltpu.VMEM((1, lanes), F32), pltpu.VMEM((1, lanes), F32),
                        pltpu.VMEM((1, lanes), F32), pltpu.VMEM((1, lanes), F32),
                        pltpu.VMEM((HEAD_DIM, lanes), F32)],
        compiler_params=_params(2),
        name="moba_attention",
    )(qkv, qkv, qkv)


def _mem_kv_kernel(mem_ref, nw_ref, w_ref, g64_ref, gain_ref, k_ref, v_ref):
    h = _rms(mem_ref[0], nw_ref[...]).astype(BF16)
    kv = _dot(h, w_ref[...])
    k = kv[:, 0:MEM_WIDTH]
    ms = _dot((k * k).astype(BF16), g64_ref[...])
    k_ref[0] = (k * lax.rsqrt(ms + EPS) * gain_ref[...]).astype(BF16)
    v_ref[0] = kv[:, MEM_WIDTH:2 * MEM_WIDTH].astype(BF16)


def _mem_kv(mem, norm_w, w_mkv, layer, k_gain):
    batch, mem_len, _ = mem.shape
    out = jax.ShapeDtypeStruct((batch, mem_len, MEM_WIDTH), BF16)
    return pl.pallas_call(
        _mem_kv_kernel,
        grid=(batch,),
        in_specs=[pl.BlockSpec((1, mem_len, D_MODEL), lambda b: (b, 0, 0)), _resident((1, D_MODEL)),
                  _layer_of(w_mkv, layer), _resident((MXU_DIM, MXU_DIM)), _resident((1, MEM_WIDTH))],
        out_specs=[pl.BlockSpec((1, mem_len, MEM_WIDTH), lambda b: (b, 0, 0))] * 2,
        out_shape=[out, out],
        compiler_params=_params(1),
        name="mem_kv",
    )(mem, norm_w.reshape(1, D_MODEL), w_mkv, _group_mean_matrix(HEAD_DIM),
      jnp.tile(k_gain, N_HEADS_MEM).reshape(1, MEM_WIDTH))


def _post_kernel(x_ref, a_ref, b_ref, c_ref, wo_ref, ncross_ref, wmq_ref, g64_ref, qgain_ref, km_ref, vm_ref,
                 wmo_ref, nffn_ref, wgu_ref, wd_ref, o_ref, *, tm):
    mix = jnp.concatenate([a_ref[...], b_ref[...], c_ref[...]], axis=1)
    x = x_ref[...] + _dot(mix, wo_ref[...])

    q = _dot(_rms(x, ncross_ref[...]).astype(BF16), wmq_ref[...])
    ms = _dot((q * q).astype(BF16), g64_ref[...])
    qn = (q * lax.rsqrt(ms + EPS) * qgain_ref[...]).astype(BF16)
    head = _lane_iota(MEM_WIDTH) >> (HEAD_DIM.bit_length() - 1)
    q4 = jnp.concatenate([jnp.where(head == n, qn, jnp.zeros_like(qn)) for n in range(N_HEADS_MEM)], axis=0)
    s = _dot_nt(q4, km_ref[0])
    e = jnp.exp(s - jnp.max(s, axis=-1, keepdims=True))
    o4 = _dot(e.astype(BF16), vm_ref[0]) / jnp.sum(e, axis=-1, keepdims=True)
    o = jnp.where(head == 0, o4[0:tm], 0.0)
    for n in range(1, N_HEADS_MEM):
        o = o + jnp.where(head == n, o4[n * tm:(n + 1) * tm], 0.0)
    x = x + _dot(o.astype(BF16), wmo_ref[...])

    h = _rms(x, nffn_ref[...]).astype(BF16)
    y = jnp.zeros((tm, D_MODEL), F32)
    for c in range(D_FF // MXU_DIM):
        lo = c * MXU_DIM
        g = _dot(h, wgu_ref[:, lo:lo + MXU_DIM])
        u = _dot(h, wgu_ref[:, D_FF + lo:D_FF + lo + MXU_DIM])
        y = y + _dot((g * jax.nn.sigmoid(g) * u).astype(BF16), wd_ref[lo:lo + MXU_DIM, :])
    o_ref[...] = x + y


def _post_attention(x2d, outs, seq, layer, w_out, norm_cross, w_mq, q_gain, k_mem, v_mem, w_mo, norm_ffn,
                    w_gate_up, w_down, tm=512):
    rows = x2d.shape[0]
    mem_len = k_mem.shape[1]
    tiles_per_seq = seq // tm
    row_spec = lambda width: pl.BlockSpec((tm, width), lambda i: (i, 0))
    mem_spec = pl.BlockSpec((1, mem_len, MEM_WIDTH), lambda i: (i // tiles_per_seq, 0, 0))
    scale = HEAD_DIM ** -0.5
    return pl.pallas_call(
        functools.partial(_post_kernel, tm=tm),
        grid=(rows // tm,),
        in_specs=[row_spec(D_MODEL), row_spec(W_DIFF), row_spec(W_DIL), row_spec(W_MOBA),
                  _layer_of(w_out, layer), _resident((1, D_MODEL)), _layer_of(w_mq, layer),
                  _resident((MXU_DIM, MXU_DIM)), _resident((1, MEM_WIDTH)), mem_spec, mem_spec,
                  _layer_of(w_mo, layer), _resident((1, D_MODEL)), _layer_of(w_gate_up, layer),
                  _layer_of(w_down, layer)],
        out_specs=row_spec(D_MODEL),
        out_shape=jax.ShapeDtypeStruct((rows, D_MODEL), F32),
        compiler_params=_params(1),
        name="post_attention",
    )(x2d, *[o.reshape(rows, -1) for o in outs], w_out, norm_cross.reshape(1, D_MODEL),
      w_mq, _group_mean_matrix(HEAD_DIM), (jnp.tile(q_gain, N_HEADS_MEM) * scale).reshape(1, MEM_WIDTH),
      k_mem, v_mem, w_mo, norm_ffn.reshape(1, D_MODEL), w_gate_up, w_down)


def kernel(x, mem, positions, norm_mix, w_in, qn_diff, kn_diff, lambda_q1, lambda_k1, lambda_q2, lambda_k2,
           subln_diff, qn_dil, kn_dil, qn_moba, kn_moba, w_out, norm_cross, norm_mem, w_mq, w_mkv, qn_mem,
           kn_mem, w_mo, norm_ffn, w_gate_up, w_down):
    batch, seq, _ = x.shape
    depth = w_in.shape[0]
    rows = batch * seq
    tables = _rope_tables(positions)
    x2d = x.reshape(rows, D_MODEL)
    w_in, w_out, w_mq, w_mkv, w_mo, w_gate_up, w_down = (
        w.astype(BF16) for w in (w_in, w_out, w_mq, w_mkv, w_mo, w_gate_up, w_down))
    for l in range(depth):
        gains = jnp.concatenate([
            jnp.tile(qn_diff[l], 2 * N_HEADS_DIFF) * (DIFF_HALF ** -0.5 * LOG2_E),
            jnp.tile(qn_dil[l], N_HEADS_DIL) * (HEAD_DIM ** -0.5 * LOG2_E),
            jnp.tile(qn_moba[l], N_HEADS_MOBA) * (HEAD_DIM ** -0.5 * LOG2_E),
            jnp.tile(kn_diff[l], 2 * N_HEADS_DIFF), jnp.tile(kn_dil[l], N_HEADS_DIL),
            jnp.tile(kn_moba[l], N_HEADS_MOBA)]).reshape(1, 2 * MIX_WIDTH)
        qkv_a, qkv_b, qkv_c = _qkv_project(x2d, norm_mix[l], w_in, l, gains, tables)
        lam_init = 0.8 - 0.6 * math.exp(-0.3 * l)
        o_a = _diff_attention(qkv_a.reshape(batch, seq, -1),
                              (lambda_q1[l], lambda_k1[l], lambda_q2[l], lambda_k2[l]), subln_diff[l], lam_init)
        o_b = _dilated_attention(qkv_b.reshape(batch, seq, -1))
        o_c = _moba_attention(qkv_c.reshape(batch, seq, -1))
        k_mem, v_mem = _mem_kv(mem, norm_mem[l], w_mkv, l, kn_mem[l])
        x2d = _post_attention(x2d, (o_a, o_b, o_c), seq, l, w_out, norm_cross[l], w_mq, qn_mem[l], k_mem, v_mem,
                              w_mo, norm_ffn[l], w_gate_up, w_down)
    return x2d.reshape(batch, seq, D_MODEL)
```

```python
import functools
import math

import jax
import jax.numpy as jnp
import numpy as np
from jax import lax
from jax.experimental import pallas as pl
from jax.experimental.pallas import tpu as pltpu

D_MODEL = 1024
HEAD_DIM = 64
N_HEADS_DIFF = 4
N_HEADS_DIL = 6
N_HEADS_MOBA = 6
DIFF_HALF = HEAD_DIM // 2
ROPE_THETA = 500000.0
ROPE_FRACTION = 4
DILATED_PAIRS = ((128, 1), (512, 4), (2048, 16))
DIL_STEPS = 128
MOBA_BLOCK = 256
MOBA_TOPK = 3
MEM_LEN = 256
N_HEADS_MEM = 4
MEM_WIDTH = N_HEADS_MEM * HEAD_DIM
D_FF = 2816
EPS = 1e-6

W_DIFF = N_HEADS_DIFF * HEAD_DIM
W_DIL = N_HEADS_DIL * HEAD_DIM
W_MOBA = N_HEADS_MOBA * HEAD_DIM
MIX_WIDTH = W_DIFF + W_DIL + W_MOBA

LANES = 128
MXU_DIM = 256
VMEM_LIMIT_BYTES = 56 * 2**20

MASKED = -1e30
LOG2_E = math.log2(math.e)

BF16 = jnp.bfloat16
F32 = jnp.float32
NT_DIMS = (((1,), (1,)), ((), ()))


def _dot(a, b):
    return jnp.dot(a, b, preferred_element_type=F32)


def _dot_nt(a, b):
    return lax.dot_general(a, b, NT_DIMS, preferred_element_type=F32)


def _params(n_grid):
    return pltpu.CompilerParams(dimension_semantics=("arbitrary",) * n_grid,
                                vmem_limit_bytes=VMEM_LIMIT_BYTES)


def _resident(shape):
    nd = len(shape)
    return pl.BlockSpec(shape, lambda *_: (0,) * nd, pipeline_mode=pl.Buffered(1))


def _layer_of(stacked, layer):
    tail = stacked.shape[1:]
    return pl.BlockSpec((None,) + tail, lambda *_: (layer,) + (0,) * len(tail), pipeline_mode=pl.Buffered(1))


def _rms(x, w):
    return x * lax.rsqrt(jnp.mean(x * x, axis=-1, keepdims=True) + EPS) * w


def _lane_iota(n=LANES):
    return lax.broadcasted_iota(jnp.int32, (1, n), 1)


N_FREQ = 16


def _rope_kernel(pos_ref, freq_ref, sel_ref, ch_ref, cd_ref, sh_ref, sd_ref):
    tm = pos_ref.shape[1]
    ang = freq_ref[...] * pos_ref[...].astype(F32)
    pad = jnp.zeros((LANES - N_FREQ, tm), F32)
    for trig, tables in ((jnp.cos, ((0, ch_ref), (1, cd_ref))), (jnp.sin, ((2, sh_ref), (3, sd_ref)))):
        x = jnp.concatenate([trig(ang), pad], axis=0).T
        hi = x.astype(BF16)
        rest = x - hi.astype(F32)
        mid = rest.astype(BF16)
        lo = (rest - mid.astype(F32)).astype(BF16)
        for k, out_ref in tables:
            out_ref[...] = _dot(hi, sel_ref[k]) + _dot(mid, sel_ref[k]) + _dot(lo, sel_ref[k])


def _rope_selectors():
    sel = np.zeros((4, LANES, LANES), np.float32)
    one_row = N_FREQ - 1
    row0 = 0
    for kind, period in ((0, HEAD_DIM), (1, DIFF_HALF)):
        half = period // ROPE_FRACTION // 2
        for lane in range(LANES):
            within = lane % period
            if within < 2 * half:
                sel[kind, row0 + within % half, lane] = 1.0
                sel[2 + kind, row0 + within % half, lane] = -1.0 if within < half else 1.0
            else:
                sel[kind, one_row, lane] = 1.0
        row0 += half
    return jnp.asarray(sel, BF16)


def _rope_tables(positions):
    rows = positions.size
    tm = 2048
    freqs = [ROPE_THETA ** (-jnp.arange(half, dtype=F32) / half)
             for half in (HEAD_DIM // ROPE_FRACTION // 2, DIFF_HALF // ROPE_FRACTION // 2)]
    freqs = jnp.concatenate(freqs + [jnp.zeros((N_FREQ - sum(f.size for f in freqs),), F32)]).reshape(N_FREQ, 1)
    table = jax.ShapeDtypeStruct((rows, LANES), F32)
    ch, cd, sh, sd = pl.pallas_call(
        _rope_kernel,
        grid=(rows // tm,),
        in_specs=[pl.BlockSpec((1, tm), lambda i: (0, i)), _resident((N_FREQ, 1)), _resident((4, LANES, LANES))],
        out_specs=[pl.BlockSpec((tm, LANES), lambda i: (i, 0))] * 4,
        out_shape=[table] * 4,
        compiler_params=_params(1),
        name="rope_tables",
    )(positions.reshape(1, rows), freqs, _rope_selectors())
    return ch, sh, cd, sd


def _group_mean_matrix(group):
    idx = jnp.arange(MXU_DIM) // group
    return jnp.where(idx[:, None] == idx[None, :], 1.0 / group, 0.0).astype(BF16)


def _rotate(t, cos, sin, half, period):
    first = (_lane_iota() & (period - 1)) < half
    partner = jnp.where(first, pltpu.roll(t, LANES - half, 1), pltpu.roll(t, half, 1))
    return t * cos + partner * sin


def _qkv_kernel(x_ref, nw_ref, w_ref, gain_ref, g32_ref, g64_ref, ch_ref, sh_ref, cd_ref, sd_ref,
                a_ref, b_ref, c_ref):
    h = _rms(x_ref[...], nw_ref[...]).astype(BF16)
    wide = 2 * MXU_DIM
    for chunk in range(3 * MIX_WIDTH // MXU_DIM):
        section, col0 = divmod(chunk * MXU_DIM, MIX_WIDTH)
        if chunk % 2 == 0:
            y_wide = _dot(h, w_ref[:, chunk * MXU_DIM:chunk * MXU_DIM + wide])
        y = y_wide[:, (chunk % 2) * MXU_DIM:(chunk % 2 + 1) * MXU_DIM]
        if section < 2:
            is_diff = col0 < W_DIFF
            g_ref = g32_ref if is_diff else g64_ref
            ms = _dot((y * y).astype(BF16), g_ref[...])
            y = y * lax.rsqrt(ms + EPS) * gain_ref[:, chunk * MXU_DIM:(chunk + 1) * MXU_DIM]
        for piece in range(MXU_DIM // LANES):
            col = col0 + piece * LANES
            t = y[:, piece * LANES:(piece + 1) * LANES]
            if section < 2:
                if col < W_DIFF:
                    t = _rotate(t, cd_ref[...], sd_ref[...], DIFF_HALF // ROPE_FRACTION // 2, DIFF_HALF)
                else:
                    t = _rotate(t, ch_ref[...], sh_ref[...], HEAD_DIM // ROPE_FRACTION // 2, HEAD_DIM)
            if col < W_DIFF:
                dst = section * W_DIFF + col
                a_ref[:, dst:dst + LANES] = t.astype(BF16)
            elif col < W_DIFF + W_DIL:
                dst = section * W_DIL + col - W_DIFF
                b_ref[:, dst:dst + LANES] = t
            else:
                dst = section * W_MOBA + col - W_DIFF - W_DIL
                c_ref[:, dst:dst + LANES] = t.astype(BF16)


def _qkv_project(x2d, norm_w, w_in, layer, gains, tables, tm=512):
    rows = x2d.shape[0]
    row_spec = lambda width: pl.BlockSpec((tm, width), lambda i: (i, 0))
    return pl.pallas_call(
        _qkv_kernel,
        grid=(rows // tm,),
        in_specs=[row_spec(D_MODEL), _resident((1, D_MODEL)), _layer_of(w_in, layer),
                  _resident((1, 2 * MIX_WIDTH)), _resident((MXU_DIM, MXU_DIM)), _resident((MXU_DIM, MXU_DIM))]
                 + [row_spec(LANES)] * 4,
        out_specs=[row_spec(3 * W_DIFF), row_spec(3 * W_DIL), row_spec(3 * W_MOBA)],
        out_shape=[jax.ShapeDtypeStruct((rows, 3 * W_DIFF), BF16),
                   jax.ShapeDtypeStruct((rows, 3 * W_DIL), F32),
                   jax.ShapeDtypeStruct((rows, 3 * W_MOBA), BF16)],
        compiler_params=_params(1),
        name="qkv_project",
    )(x2d, norm_w.reshape(1, D_MODEL), w_in, gains,
      _group_mean_matrix(DIFF_HALF), _group_mean_matrix(HEAD_DIM), *tables)


def _reduce_rows(x, pair_op, final_op):
    n = x.shape[0]
    while n > 8 and n % 2 == 0:
        n //= 2
        x = pair_op(x[:n], x[n:])
    return final_op(x, axis=0, keepdims=True)


ONES_ROWS = 16


def _store_scores(s_t, buf, cols):
    s_ref, smax_ref = buf
    s_ref[:, cols] = s_t
    smax_ref[:, cols] = _reduce_rows(s_t, jnp.maximum, jnp.max)


def _softmax_update_t(buf, v_ts, m_ref, l_ref, acc_ref, start=0, stop=None):
    s_ref, smax_ref = buf
    stop = s_ref.shape[1] if stop is None else stop
    lanes = slice(start, stop)
    width = (stop - start) // len(v_ts)
    m_old = m_ref[:, lanes]
    m_new = jnp.maximum(m_old, smax_ref[:, lanes])
    alpha = jnp.exp2(m_old - m_new)
    e = jnp.exp2(s_ref[:, lanes] - m_new).astype(BF16)
    for h, v_t in enumerate(v_ts):
        cols = slice(start + h * width, start + (h + 1) * width)
        rel = slice(h * width, (h + 1) * width)
        pv = _dot(v_t, e[:, rel])
        acc_ref[:, cols] = alpha[:, rel] * acc_ref[:, cols] + pv[0:HEAD_DIM]
        l_ref[:, cols] = alpha[:, rel] * l_ref[:, cols] + pv[HEAD_DIM:HEAD_DIM + 1]
    m_ref[:, lanes] = m_new


def _transpose_values(v_ref, vt_ref):
    n_head, n_chunk, _, tk = vt_ref.shape

    def one_chunk(c, carry):
        rows = pl.ds(pl.multiple_of(c * tk, tk), tk)
        for p in range(n_head // 2):
            pair_t = v_ref[0, rows, p * LANES:(p + 1) * LANES].astype(F32).T.astype(BF16)
            for h in range(2):
                vt_ref[2 * p + h, c, 0:HEAD_DIM, :] = pair_t[h * HEAD_DIM:(h + 1) * HEAD_DIM]
                vt_ref[2 * p + h, c, HEAD_DIM:HEAD_DIM + ONES_ROWS, :] = jnp.ones((ONES_ROWS, tk), BF16)
        return carry

    lax.fori_loop(0, n_chunk, one_chunk, 0)


def _diff_kernel(lq1_ref, lk1_ref, lq2_ref, lk2_ref, subln_ref, q_ref, k_ref, v_ref, o_ref,
                 vt_ref, q4t_ref, s0_ref, s1_ref, smax0_ref, smax1_ref, m_ref, l_ref, acc_ref, *, tq, tk, lam_init):
    i = pl.program_id(1)
    n_pair = vt_ref.shape[0] // 2
    width = 4 * tq
    pair_cols = [slice(p * width, (p + 1) * width) for p in range(n_pair)]
    pair_dims = [slice(p * LANES, (p + 1) * LANES) for p in range(n_pair)]

    @pl.when(i == 0)
    def _():
        _transpose_values(v_ref, vt_ref)

    dim = lax.broadcasted_iota(jnp.int32, (LANES, 1), 0)
    for p in range(n_pair):
        q_t = q_ref[0, :, pair_dims[p]].astype(F32).T
        for n in range(4):
            keep = (dim >= n * DIFF_HALF) & (dim < (n + 1) * DIFF_HALF)
            q4t_ref[:, p * width + n * tq:p * width + (n + 1) * tq] = jnp.where(keep, q_t, 0.0).astype(BF16)
    m_ref[...] = jnp.full(m_ref.shape, MASKED, F32)
    l_ref[...] = jnp.zeros(l_ref.shape, F32)
    acc_ref[...] = jnp.zeros(acc_ref.shape, F32)

    head_width = width // 2

    def scores(j, s_ref, masked, heads=range(2 * n_pair)):
        rows = pl.ds(pl.multiple_of(j * tk, tk), tk)
        if masked:
            key_pos = lax.broadcasted_iota(jnp.int32, (tk, head_width), 0) + j * tk
            query_pos = (lax.broadcasted_iota(jnp.int32, (tk, head_width), 1) & (tq - 1)) + i * tq
            causal = key_pos <= query_pos
        for h in heads:
            cols = slice(h * head_width, (h + 1) * head_width)
            s_t = _dot(k_ref[0, rows, pair_dims[h // 2]], q4t_ref[:, cols])
            _store_scores(jnp.where(causal, s_t, -jnp.inf) if masked else s_t, s_ref, cols)

    def update(j, s_ref, next_j=None, next_ref=None):
        for h in range(2 * n_pair):
            if next_ref is not None:
                scores(next_j, next_ref, False, heads=(h,))
            _softmax_update_t(s_ref, [vt_ref[h, j]], m_ref, l_ref, acc_ref, h * head_width, (h + 1) * head_width)

    s0_ref, s1_ref = (s0_ref, smax0_ref), (s1_ref, smax1_ref)

    n_full = (i * tq) // tk
    scores(n_full, s0_ref, True)

    def chunk_pair(u, carry):
        update(jnp.where(u == 0, n_full, 2 * u - 1), s0_ref, 2 * u, s1_ref)
        update(2 * u, s1_ref, 2 * u + 1, s0_ref)
        return carry

    n_pairs = n_full // 2
    lax.fori_loop(0, n_pairs, chunk_pair, 0)
    pending = jnp.where(n_pairs == 0, n_full, 2 * n_pairs - 1)

    @pl.when(n_full % 2 == 1)
    def _():
        update(pending, s0_ref, n_full - 1, s1_ref)
        update(n_full - 1, s1_ref)

    @pl.when(n_full % 2 == 0)
    def _():
        update(pending, s0_ref)

    lam = (jnp.exp(jnp.sum(lq1_ref[...] * lk1_ref[...], axis=-1, keepdims=True))
           - jnp.exp(jnp.sum(lq2_ref[...] * lk2_ref[...], axis=-1, keepdims=True)) + lam_init)
    for p in range(n_pair):
        o4 = acc_ref[:, pair_cols[p]] / l_ref[:, pair_cols[p]]
        heads = []
        for h in range(2):
            o = o4[:, 2 * h * tq:(2 * h + 1) * tq] - lam * o4[:, (2 * h + 1) * tq:(2 * h + 2) * tq]
            heads.append(o * lax.rsqrt(jnp.mean(o * o, axis=0, keepdims=True) + EPS))
        o = jnp.concatenate(heads, axis=0) * subln_ref[...] * (1.0 - lam_init)
        o_ref[0, :, pair_dims[p]] = o.T.astype(o_ref.dtype)


def _diff_attention(qkv, lam_vecs, subln, lam_init, tq=256, tk=512):
    batch, seq, _ = qkv.shape
    n_pair = W_DIFF // LANES
    vec = _resident((1, DIFF_HALF))
    kernel = functools.partial(_diff_kernel, tq=tq, tk=tk, lam_init=lam_init)
    lanes = n_pair * 4 * tq
    return pl.pallas_call(
        kernel,
        grid=(batch, seq // tq),
        in_specs=[vec, vec, vec, vec, _resident((LANES, 1)),
                  pl.BlockSpec((1, tq, W_DIFF), lambda b, i: (b, i, 0)),
                  pl.BlockSpec((1, seq, W_DIFF), lambda b, i: (b, 0, 1)),
                  pl.BlockSpec((1, seq, W_DIFF), lambda b, i: (b, 0, 2))],
        out_specs=pl.BlockSpec((1, tq, W_DIFF), lambda b, i: (b, i, 0)),
        out_shape=jax.ShapeDtypeStruct((batch, seq, W_DIFF), BF16),
        scratch_shapes=[pltpu.VMEM((2 * n_pair, seq // tk, HEAD_DIM + ONES_ROWS, tk), BF16),
                        pltpu.VMEM((LANES, lanes), BF16),
                        pltpu.VMEM((tk, lanes), F32), pltpu.VMEM((tk, lanes), F32),
                        pltpu.VMEM((1, lanes), F32), pltpu.VMEM((1, lanes), F32),
                        pltpu.VMEM((1, lanes), F32), pltpu.VMEM((1, lanes), F32),
                        pltpu.VMEM((HEAD_DIM, lanes), F32)],
        compiler_params=_params(2),
        name="diff_attention",
    )(*[v.reshape(1, DIFF_HALF) for v in lam_vecs], jnp.tile(subln, 2).reshape(LANES, 1), qkv, qkv, qkv)


def _dil_kernel(q_ref, k_ref, v_ref, o_ref, o0_ref, o1_ref, o2_ref, l0_ref, l1_ref, l2_ref, *, tile, group):
    base = pl.program_id(2) * tile
    w = DIL_STEPS
    lane = _lane_iota()
    qi = lax.broadcasted_iota(jnp.int32, (2 * w, 2 * w), 0) & (w - 1)
    kj = lax.broadcasted_iota(jnp.int32, (2 * w, 2 * w), 1)
    band_bias = jnp.where((kj >= qi) & (kj <= qi + w), 0.0, -jnp.inf)

    for (window, d), ob_ref, lse_ref in zip(DILATED_PAIRS, (o0_ref, o1_ref, o2_ref), (l0_ref, l1_ref, l2_ref)):
        def rows(ref, start, d=d):
            if d == 1:
                return ref[0, pl.ds(start, w), :]
            return ref[0, pl.ds(start, w, stride=d), :]

        def block_group(grp, carry, window=window, d=d, ob_ref=ob_ref, lse_ref=lse_ref, rows=rows):
            local, prev, own, no_prev_bias, s, e, m, o = {}, {}, {}, {}, {}, {}, {}, {}

            def stage_scores(g):
                blk = grp * group + g
                local[g] = (blk // d) * window + (blk & (d - 1))
                own[g] = base + local[g]
                has_prev = own[g] >= window
                prev[g] = jnp.where(has_prev, own[g] - window, own[g])
                no_prev_bias[g] = jnp.where(has_prev, 0.0, -jnp.inf)
                q = rows(q_ref, local[g])
                q2 = jnp.concatenate([jnp.where(lane < HEAD_DIM, q, 0.0), jnp.where(lane < HEAD_DIM, 0.0, q)], axis=0)
                k2 = jnp.concatenate([rows(k_ref, prev[g]), rows(k_ref, own[g])], axis=0)
                s[g] = _dot_nt(q2.astype(BF16), k2.astype(BF16))

            def stage_softmax(g):
                sg = s.pop(g) + band_bias
                sg = jnp.concatenate([sg[:, 0:w] + no_prev_bias[g], sg[:, w:2 * w]], axis=1)
                m[g] = jnp.max(sg, axis=-1, keepdims=True)
                e[g] = jnp.exp2(sg - m[g]).astype(BF16)

            def stage_values(g):
                v2 = jnp.concatenate([rows(v_ref, prev[g]), rows(v_ref, own[g])], axis=0).astype(BF16)
                o[g] = _dot(e.pop(g), jnp.concatenate([v2, jnp.ones_like(v2)], axis=1))

            def stage_output(g):
                first = lane < HEAD_DIM
                og = o.pop(g)
                l = jnp.where(first, og[0:w, LANES:2 * LANES], og[w:2 * w, LANES:2 * LANES])
                ob = jnp.where(first, og[0:w, 0:LANES], og[w:2 * w, 0:LANES]) / l
                lb = jnp.where(first, m[g][0:w], m[g][w:2 * w]) + jnp.log2(l)
                if d == 1:
                    ob_ref[pl.ds(local[g], w), :] = ob
                    lse_ref[pl.ds(local[g], w), :] = lb
                else:
                    ob_ref[pl.ds(local[g], w, stride=d), :] = ob
                    lse_ref[pl.ds(local[g], w, stride=d), :] = lb

            stages = (stage_scores, stage_softmax, stage_values, stage_output)
            for step in range(group + len(stages) - 1):
                for lag, stage in enumerate(stages):
                    if 0 <= step - lag < group:
                        stage(step - lag)
            return carry

        lax.fori_loop(0, tile // w // group, block_group, 0)

    lse = [l0_ref[...], l1_ref[...], l2_ref[...]]
    m = jnp.maximum(jnp.maximum(lse[0], lse[1]), lse[2])
    wts = [jnp.exp2(x - m) for x in lse]
    mixed = wts[0] * o0_ref[...] + wts[1] * o1_ref[...] + wts[2] * o2_ref[...]
    o_ref[0] = (mixed / (wts[0] + wts[1] + wts[2])).astype(o_ref.dtype)


def _dilated_attention(qkv, group=16):
    batch, seq, _ = qkv.shape
    n_pairs = W_DIL // LANES
    tile = max(window for window, _ in DILATED_PAIRS)
    assert seq % tile == 0
    return pl.pallas_call(
        functools.partial(_dil_kernel, tile=tile, group=group),
        grid=(batch, n_pairs, seq // tile),
        in_specs=[pl.BlockSpec((1, tile, LANES), lambda b, p, i: (b, i, p)),
                  pl.BlockSpec((1, seq, LANES), lambda b, p, i: (b, 0, n_pairs + p)),
                  pl.BlockSpec((1, seq, LANES), lambda b, p, i: (b, 0, 2 * n_pairs + p))],
        out_specs=pl.BlockSpec((1, tile, LANES), lambda b, p, i: (b, i, p)),
        out_shape=jax.ShapeDtypeStruct((batch, seq, W_DIL), BF16),
        scratch_shapes=[pltpu.VMEM((tile, LANES), F32)] * 6,
        compiler_params=_params(3),
        name="dilated_attention",
    )(qkv, qkv, qkv)


def _moba_kernel(q_ref, k_ref, v_ref, o_ref, kaug_ref, vt_ref, kmh_ref, kml_ref, qaugt_ref, so_ref, s0_ref, s1_ref,
                 smaxo_ref, smax0_ref, smax1_ref, m_ref, l_ref, acc_ref):
    i = pl.program_id(1)
    blk = MOBA_BLOCK
    tk = 2 * blk
    blk_shift = blk.bit_length() - 1
    seq = k_ref.shape[1]
    n_blk = seq // blk
    n_pair = vt_ref.shape[0] // 2
    width = 2 * blk
    pair_cols = [slice(p * width, (p + 1) * width) for p in range(n_pair)]
    pair_dims = [slice(p * LANES, (p + 1) * LANES) for p in range(n_pair)]

    @pl.when(i == 0)
    def _():
        qaugt_ref[LANES:2 * LANES, :] = jnp.zeros((LANES, n_pair * width), BF16)
        _transpose_values(v_ref, vt_ref)
        piece = 1024
        for p in range(n_pair):
            def one_piece(c, total, p=p):
                rows = pl.ds(pl.multiple_of(c * piece, piece), piece)
                kc = k_ref[0, rows, pair_dims[p]]
                key_blk = (lax.broadcasted_iota(jnp.int32, (piece, LANES), 0) + c * piece) >> blk_shift
                kaug_ref[p, rows, 0:LANES] = kc
                kaug_ref[p, rows, LANES:2 * LANES] = (
                    lax.broadcasted_iota(jnp.int32, (piece, LANES), 1) == key_blk).astype(BF16)
                member = (lax.broadcasted_iota(jnp.int32, (LANES, piece), 0)
                          == (lax.broadcasted_iota(jnp.int32, (LANES, piece), 1) + c * piece) >> blk_shift)
                return total + _dot(member.astype(BF16), kc)

            total = lax.fori_loop(0, seq // piece, one_piece, jnp.zeros((LANES, LANES), F32))
            k_mean = total * (1.0 / blk)
            hi = k_mean.astype(BF16)
            kmh_ref[p] = hi
            kml_ref[p] = (k_mean - hi.astype(F32)).astype(BF16)

    dim = lax.broadcasted_iota(jnp.int32, (LANES, 1), 0)
    q2t, gates = [], []
    for p in range(n_pair):
        q_t = q_ref[0, :, pair_dims[p]].astype(F32).T
        q2t.append(jnp.concatenate([jnp.where(dim < HEAD_DIM, q_t, 0.0), jnp.where(dim < HEAD_DIM, 0.0, q_t)],
                                   axis=1).astype(BF16))
        gates.append(_dot(kmh_ref[p, 0:n_blk, :], q2t[p]) + _dot(kml_ref[p, 0:n_blk, :], q2t[p]))

    gate = jnp.concatenate(gates, axis=1)
    blk_id = lax.broadcasted_iota(jnp.int32, gate.shape, 0)
    blk_f = blk_id.astype(F32)
    g = jnp.where(blk_id < i, gate, -jnp.inf)
    selected = jnp.zeros(gate.shape, F32)
    for _ in range(MOBA_TOPK):
        best = jnp.max(g, axis=0, keepdims=True)
        first = jnp.min(jnp.where(g == best, blk_f, float(LANES)), axis=0, keepdims=True)
        pick = (blk_f == first) & (best > -jnp.inf)
        selected = jnp.where(pick, 1.0, selected)
        g = jnp.where(pick, -jnp.inf, g)
    for p in range(n_pair):
        qaugt_ref[0:LANES, pair_cols[p]] = q2t[p]
    qaugt_ref[LANES:LANES + n_blk, :] = jnp.where(selected > 0.0, 0.0, MASKED).astype(BF16)

    m_ref[...] = jnp.full(m_ref.shape, MASKED, F32)
    l_ref[...] = jnp.zeros(l_ref.shape, F32)
    acc_ref[...] = jnp.zeros(acc_ref.shape, F32)

    def scores(c, s_ref, pairs=range(n_pair)):
        rows = pl.ds(pl.multiple_of(c * tk, tk), tk)
        for p in pairs:
            _store_scores(_dot(kaug_ref[p, rows, :], qaugt_ref[:, pair_cols[p]]), s_ref, pair_cols[p])

    def update(c, s_ref, next_c=None, next_ref=None):
        for p in range(n_pair):
            if next_ref is not None:
                scores(next_c, next_ref, pairs=(p,))
            v_ts = [jnp.concatenate([vt_ref[2 * p + h, 2 * c], vt_ref[2 * p + h, 2 * c + 1]], axis=1) for h in range(2)]
            _softmax_update_t(s_ref, v_ts, m_ref, l_ref, acc_ref, p * width, (p + 1) * width)

    so_ref, s0_ref, s1_ref = (so_ref, smaxo_ref), (s0_ref, smax0_ref), (s1_ref, smax1_ref)

    own_rows = pl.ds(pl.multiple_of(i * blk, blk), blk)
    key = lax.broadcasted_iota(jnp.int32, (blk, width), 0)
    query = lax.broadcasted_iota(jnp.int32, (blk, width), 1) & (blk - 1)
    for p in range(n_pair):
        s_own = _dot(kaug_ref[p, own_rows, 0:LANES], q2t[p])
        _store_scores(jnp.where(key <= query, s_own, -jnp.inf), so_ref, pair_cols[p])
    for p in range(n_pair):
        scores(0, s0_ref, pairs=(p,))
        _softmax_update_t(so_ref, [vt_ref[2 * p + h, i] for h in range(2)], m_ref, l_ref, acc_ref,
                          p * width, (p + 1) * width)

    n = jnp.maximum((i + 1) // 2, 1)

    def chunk_pair(u, carry):
        update(2 * u, s0_ref, 2 * u + 1, s1_ref)
        update(2 * u + 1, s1_ref, 2 * u + 2, s0_ref)
        return carry

    lax.fori_loop(0, (n - 1) // 2, chunk_pair, 0)

    @pl.when(n % 2 == 1)
    def _():
        update(n - 1, s0_ref)

    @pl.when(n % 2 == 0)
    def _():
        update(n - 2, s0_ref, n - 1, s1_ref)
        update(n - 1, s1_ref)

    for p in range(n_pair):
        o = acc_ref[:, pair_cols[p]] / l_ref[:, pair_cols[p]]
        o_ref[0, :, pair_dims[p]] = jnp.concatenate([o[:, 0:blk], o[:, blk:2 * blk]], axis=0).T.astype(o_ref.dtype)


def _moba_attention(qkv):
    batch, seq, _ = qkv.shape
    n_pair = W_MOBA // LANES
    blk = MOBA_BLOCK
    lanes = n_pair * 2 * blk
    assert seq % 1024 == 0 and seq // blk <= LANES and seq // blk >= MOBA_TOPK
    assert (seq // blk) % 16 == 0
    whole = lambda j: pl.BlockSpec((1, seq, W_MOBA), lambda b, i: (b, 0, j), pipeline_mode=pl.Buffered(1))
    return pl.pallas_call(
        _moba_kernel,
        grid=(batch, seq // blk),
        in_specs=[pl.BlockSpec((1, blk, W_MOBA), lambda b, i: (b, i, 0)), whole(1), whole(2)],
        out_specs=pl.BlockSpec((1, blk, W_MOBA), lambda b, i: (b, i, 0)),
        out_shape=jax.ShapeDtypeStruct((batch, seq, W_MOBA), BF16),
        scratch_shapes=[pltpu.VMEM((n_pair, seq, 2 * LANES), BF16),
                        pltpu.VMEM((2 * n_pair, seq // blk, HEAD_DIM + ONES_ROWS, blk), BF16),
                        pltpu.VMEM((n_pair, LANES, LANES), BF16), pltpu.VMEM((n_pair, LANES, LANES), BF16),
                        pltpu.VMEM((2 * LANES, lanes), BF16), pltpu.VMEM((blk, lanes), F32),
                        pltpu.VMEM((2 * blk, lanes), F32), pltpu.VMEM((2 * blk, lanes), F32),
                        pltpu.VMEM((1, lanes), F32), pltpu.VMEM((1, lanes), F32), pltpu.VMEM((1, lanes), F32),
                        pltpu.VMEM((1, lanes), F32), pltpu.VMEM((1, lanes), F32),
                        pltpu.VMEM((HEAD_DIM, lanes), F32)],
        compiler_params=_params(2),
        name="moba_attention",
    )(qkv, qkv, qkv)


def _mem_kv_kernel(mem_ref, nw_ref, w_ref, g64_ref, gain_ref, k_ref, v_ref):
    h = _rms(mem_ref[0], nw_ref[...]).astype(BF16)
    kv = _dot(h, w_ref[...])
    k = kv[:, 0:MEM_WIDTH]
    ms = _dot((k * k).astype(BF16), g64_ref[...])
    k_ref[0] = (k * lax.rsqrt(ms + EPS) * gain_ref[...]).astype(BF16)
    v_ref[0] = kv[:, MEM_WIDTH:2 * MEM_WIDTH].astype(BF16)


def _mem_kv(mem, norm_w, w_mkv, layer, k_gain):
    batch, mem_len, _ = mem.shape
    out = jax.ShapeDtypeStruct((batch, mem_len, MEM_WIDTH), BF16)
    return pl.pallas_call(
        _mem_kv_kernel,
        grid=(batch,),
        in_specs=[pl.BlockSpec((1, mem_len, D_MODEL), lambda b: (b, 0, 0)), _resident((1, D_MODEL)),
                  _layer_of(w_mkv, layer), _resident((MXU_DIM, MXU_DIM)), _resident((1, MEM_WIDTH))],
        out_specs=[pl.BlockSpec((1, mem_len, MEM_WIDTH), lambda b: (b, 0, 0))] * 2,
        out_shape=[out, out],
        compiler_params=_params(1),
        name="mem_kv",
    )(mem, norm_w.reshape(1, D_MODEL), w_mkv, _group_mean_matrix(HEAD_DIM),
      jnp.tile(k_gain, N_HEADS_MEM).reshape(1, MEM_WIDTH))


def _post_kernel(x_ref, a_ref, b_ref, c_ref, wo_ref, ncross_ref, wmq_ref, g64_ref, qgain_ref, km_ref, vm_ref,
                 wmo_ref, nffn_ref, wgu_ref, wd_ref, o_ref, *, tm):
    mix = jnp.concatenate([a_ref[...], b_ref[...], c_ref[...]], axis=1)
    x = x_ref[...] + _dot(mix, wo_ref[...])

    q = _dot(_rms(x, ncross_ref[...]).astype(BF16), wmq_ref[...])
    ms = _dot((q * q).astype(BF16), g64_ref[...])
    qn = (q * lax.rsqrt(ms + EPS) * qgain_ref[...]).astype(BF16)
    head = _lane_iota(MEM_WIDTH) >> (HEAD_DIM.bit_length() - 1)
    q4 = jnp.concatenate([jnp.where(head == n, qn, jnp.zeros_like(qn)) for n in range(N_HEADS_MEM)], axis=0)
    s = _dot_nt(q4, km_ref[0])
    e = jnp.exp(s - jnp.max(s, axis=-1, keepdims=True))
    o4 = _dot(e.astype(BF16), vm_ref[0]) / jnp.sum(e, axis=-1, keepdims=True)
    o = jnp.where(head == 0, o4[0:tm], 0.0)
    for n in range(1, N_HEADS_MEM):
        o = o + jnp.where(head == n, o4[n * tm:(n + 1) * tm], 0.0)
    x = x + _dot(o.astype(BF16), wmo_ref[...])

    h = _rms(x, nffn_ref[...]).astype(BF16)
    y = jnp.zeros((tm, D_MODEL), F32)
    for c in range(D_FF // MXU_DIM):
        lo = c * MXU_DIM
        g = _dot(h, wgu_ref[:, lo:lo + MXU_DIM])
        u = _dot(h, wgu_ref[:, D_FF + lo:D_FF + lo + MXU_DIM])
        y = y + _dot((g * jax.nn.sigmoid(g) * u).astype(BF16), wd_ref[lo:lo + MXU_DIM, :])
    o_ref[...] = x + y


def _post_attention(x2d, outs, seq, layer, w_out, norm_cross, w_mq, q_gain, k_mem, v_mem, w_mo, norm_ffn,
                    w_gate_up, w_down, tm=512):
    rows = x2d.shape[0]
    mem_len = k_mem.shape[1]
    tiles_per_seq = seq // tm
    row_spec = lambda width: pl.BlockSpec((tm, width), lambda i: (i, 0))
    mem_spec = pl.BlockSpec((1, mem_len, MEM_WIDTH), lambda i: (i // tiles_per_seq, 0, 0))
    scale = HEAD_DIM ** -0.5
    return pl.pallas_call(
        functools.partial(_post_kernel, tm=tm),
        grid=(rows // tm,),
        in_specs=[row_spec(D_MODEL), row_spec(W_DIFF), row_spec(W_DIL), row_spec(W_MOBA),
                  _layer_of(w_out, layer), _resident((1, D_MODEL)), _layer_of(w_mq, layer),
                  _resident((MXU_DIM, MXU_DIM)), _resident((1, MEM_WIDTH)), mem_spec, mem_spec,
                  _layer_of(w_mo, layer), _resident((1, D_MODEL)), _layer_of(w_gate_up, layer),
                  _layer_of(w_down, layer)],
        out_specs=row_spec(D_MODEL),
        out_shape=jax.ShapeDtypeStruct((rows, D_MODEL), F32),
        compiler_params=_params(1),
        name="post_attention",
    )(x2d, *[o.reshape(rows, -1) for o in outs], w_out, norm_cross.reshape(1, D_MODEL),
      w_mq, _group_mean_matrix(HEAD_DIM), (jnp.tile(q_gain, N_HEADS_MEM) * scale).reshape(1, MEM_WIDTH),
      k_mem, v_mem, w_mo, norm_ffn.reshape(1, D_MODEL), w_gate_up, w_down)


def kernel(x, mem, positions, norm_mix, w_in, qn_diff, kn_diff, lambda_q1, lambda_k1, lambda_q2, lambda_k2,
           subln_diff, qn_dil, kn_dil, qn_moba, kn_moba, w_out, norm_cross, norm_mem, w_mq, w_mkv, qn_mem,
           kn_mem, w_mo, norm_ffn, w_gate_up, w_down):
    batch, seq, _ = x.shape
    depth = w_in.shape[0]
    rows = batch * seq
    tables = _rope_tables(positions)
    x2d = x.reshape(rows, D_MODEL)
    w_in, w_out, w_mq, w_mkv, w_mo, w_gate_up, w_down = (
        w.astype(BF16) for w in (w_in, w_out, w_mq, w_mkv, w_mo, w_gate_up, w_down))
    for l in range(depth):
        gains = jnp.concatenate([
            jnp.tile(qn_diff[l], 2 * N_HEADS_DIFF) * (DIFF_HALF ** -0.5 * LOG2_E),
            jnp.tile(qn_dil[l], N_HEADS_DIL) * (HEAD_DIM ** -0.5 * LOG2_E),
            jnp.tile(qn_moba[l], N_HEADS_MOBA) * (HEAD_DIM ** -0.5 * LOG2_E),
            jnp.tile(kn_diff[l], 2 * N_HEADS_DIFF), jnp.tile(kn_dil[l], N_HEADS_DIL),
            jnp.tile(kn_moba[l], N_HEADS_MOBA)]).reshape(1, 2 * MIX_WIDTH)
        qkv_a, qkv_b, qkv_c = _qkv_project(x2d, norm_mix[l], w_in, l, gains, tables)
        lam_init = 0.8 - 0.6 * math.exp(-0.3 * l)
        o_a = _diff_attention(qkv_a.reshape(batch, seq, -1),
                              (lambda_q1[l], lambda_k1[l], lambda_q2[l], lambda_k2[l]), subln_diff[l], lam_init)
        o_b = _dilated_attention(qkv_b.reshape(batch, seq, -1))
        o_c = _moba_attention(qkv_c.reshape(batch, seq, -1))
        k_mem, v_mem = _mem_kv(mem, norm_mem[l], w_mkv, l, kn_mem[l])
        x2d = _post_attention(x2d, (o_a, o_b, o_c), seq, l, w_out, norm_cross[l], w_mq, qn_mem[l], k_mem, v_mem,
                              w_mo, norm_ffn[l], w_gate_up, w_down)
    return x2d.reshape(batch, seq, D_MODEL)
```

```python
import functools
import math

import jax
import jax.numpy as jnp
import numpy as np
from jax import lax
from jax.experimental import pallas as pl
from jax.experimental.pallas import tpu as pltpu

D_MODEL = 1024
HEAD_DIM = 64
N_HEADS_DIFF = 4
N_HEADS_DIL = 6
N_HEADS_MOBA = 6
DIFF_HALF = HEAD_DIM // 2
ROPE_THETA = 500000.0
ROPE_FRACTION = 4
DILATED_PAIRS = ((128, 1), (512, 4), (2048, 16))
DIL_STEPS = 128
MOBA_BLOCK = 256
MOBA_TOPK = 3
MEM_LEN = 256
N_HEADS_MEM = 4
MEM_WIDTH = N_HEADS_MEM * HEAD_DIM
D_FF = 2816
EPS = 1e-6

W_DIFF = N_HEADS_DIFF * HEAD_DIM
W_DIL = N_HEADS_DIL * HEAD_DIM
W_MOBA = N_HEADS_MOBA * HEAD_DIM
MIX_WIDTH = W_DIFF + W_DIL + W_MOBA

LANES = 128
MXU_DIM = 256
VMEM_LIMIT_BYTES = 56 * 2**20

MASKED = -1e30
LOG2_E = math.log2(math.e)

BF16 = jnp.bfloat16
F32 = jnp.float32
NT_DIMS = (((1,), (1,)), ((), ()))


def _dot(a, b):
    return jnp.dot(a, b, preferred_element_type=F32)


def _dot_nt(a, b):
    return lax.dot_general(a, b, NT_DIMS, preferred_element_type=F32)


def _params(n_grid):
    return pltpu.CompilerParams(dimension_semantics=("arbitrary",) * n_grid,
                                vmem_limit_bytes=VMEM_LIMIT_BYTES)


def _resident(shape):
    nd = len(shape)
    return pl.BlockSpec(shape, lambda *_: (0,) * nd, pipeline_mode=pl.Buffered(1))


def _layer_of(stacked, layer):
    tail = stacked.shape[1:]
    return pl.BlockSpec((None,) + tail, lambda *_: (layer,) + (0,) * len(tail), pipeline_mode=pl.Buffered(1))


def _rms(x, w):
    return x * lax.rsqrt(jnp.mean(x * x, axis=-1, keepdims=True) + EPS) * w


def _lane_iota(n=LANES):
    return lax.broadcasted_iota(jnp.int32, (1, n), 1)


N_FREQ = 16


def _rope_kernel(pos_ref, freq_ref, sel_ref, ch_ref, cd_ref, sh_ref, sd_ref):
    tm = pos_ref.shape[1]
    ang = freq_ref[...] * pos_ref[...].astype(F32)
    pad = jnp.zeros((LANES - N_FREQ, tm), F32)
    for trig, tables in ((jnp.cos, ((0, ch_ref), (1, cd_ref))), (jnp.sin, ((2, sh_ref), (3, sd_ref)))):
        x = jnp.concatenate([trig(ang), pad], axis=0).T
        hi = x.astype(BF16)
        rest = x - hi.astype(F32)
        mid = rest.astype(BF16)
        lo = (rest - mid.astype(F32)).astype(BF16)
        for k, out_ref in tables:
            out_ref[...] = _dot(hi, sel_ref[k]) + _dot(mid, sel_ref[k]) + _dot(lo, sel_ref[k])


def _rope_selectors():
    sel = np.zeros((4, LANES, LANES), np.float32)
    one_row = N_FREQ - 1
    row0 = 0
    for kind, period in ((0, HEAD_DIM), (1, DIFF_HALF)):
        half = period // ROPE_FRACTION // 2
        for lane in range(LANES):
            within = lane % period
            if within < 2 * half:
                sel[kind, row0 + within % half, lane] = 1.0
                sel[2 + kind, row0 + within % half, lane] = -1.0 if within < half else 1.0
            else:
                sel[kind, one_row, lane] = 1.0
        row0 += half
    return jnp.asarray(sel, BF16)


def _rope_tables(positions):
    rows = positions.size
    tm = 2048
    freqs = [ROPE_THETA ** (-jnp.arange(half, dtype=F32) / half)
             for half in (HEAD_DIM // ROPE_FRACTION // 2, DIFF_HALF // ROPE_FRACTION // 2)]
    freqs = jnp.concatenate(freqs + [jnp.zeros((N_FREQ - sum(f.size for f in freqs),), F32)]).reshape(N_FREQ, 1)
    table = jax.ShapeDtypeStruct((rows, LANES), F32)
    ch, cd, sh, sd = pl.pallas_call(
        _rope_kernel,
        grid=(rows // tm,),
        in_specs=[pl.BlockSpec((1, tm), lambda i: (0, i)), _resident((N_FREQ, 1)), _resident((4, LANES, LANES))],
        out_specs=[pl.BlockSpec((tm, LANES), lambda i: (i, 0))] * 4,
        out_shape=[table] * 4,
        compiler_params=_params(1),
        name="rope_tables",
    )(positions.reshape(1, rows), freqs, _rope_selectors())
    return ch, sh, cd, sd


def _group_mean_matrix(group):
    idx = jnp.arange(MXU_DIM) // group
    return jnp.where(idx[:, None] == idx[None, :], 1.0 / group, 0.0).astype(BF16)


def _rotate(t, cos, sin, half, period):
    first = (_lane_iota() & (period - 1)) < half
    partner = jnp.where(first, pltpu.roll(t, LANES - half, 1), pltpu.roll(t, half, 1))
    return t * cos + partner * sin


def _qkv_kernel(x_ref, nw_ref, w_ref, gain_ref, g32_ref, g64_ref, ch_ref, sh_ref, cd_ref, sd_ref,
                a_ref, b_ref, c_ref):
    h = _rms(x_ref[...], nw_ref[...]).astype(BF16)
    wide = 2 * MXU_DIM
    for chunk in range(3 * MIX_WIDTH // MXU_DIM):
        section, col0 = divmod(chunk * MXU_DIM, MIX_WIDTH)
        if chunk % 2 == 0:
            y_wide = _dot(h, w_ref[:, chunk * MXU_DIM:chunk * MXU_DIM + wide])
        y = y_wide[:, (chunk % 2) * MXU_DIM:(chunk % 2 + 1) * MXU_DIM]
        if section < 2:
            is_diff = col0 < W_DIFF
            g_ref = g32_ref if is_diff else g64_ref
            ms = _dot((y * y).astype(BF16), g_ref[...])
            y = y * lax.rsqrt(ms + EPS) * gain_ref[:, chunk * MXU_DIM:(chunk + 1) * MXU_DIM]
        for piece in range(MXU_DIM // LANES):
            col = col0 + piece * LANES
            t = y[:, piece * LANES:(piece + 1) * LANES]
            if section < 2:
                if col < W_DIFF:
                    t = _rotate(t, cd_ref[...], sd_ref[...], DIFF_HALF // ROPE_FRACTION // 2, DIFF_HALF)
                else:
                    t = _rotate(t, ch_ref[...], sh_ref[...], HEAD_DIM // ROPE_FRACTION // 2, HEAD_DIM)
            if col < W_DIFF:
                dst = section * W_DIFF + col
                a_ref[:, dst:dst + LANES] = t.astype(BF16)
            elif col < W_DIFF + W_DIL:
                dst = section * W_DIL + col - W_DIFF
                b_ref[:, dst:dst + LANES] = t
            else:
                dst = section * W_MOBA + col - W_DIFF - W_DIL
                c_ref[:, dst:dst + LANES] = t.astype(BF16)


def _qkv_project(x2d, norm_w, w_in, layer, gains, tables, tm=512):
    rows = x2d.shape[0]
    row_spec = lambda width: pl.BlockSpec((tm, width), lambda i: (i, 0))
    return pl.pallas_call(
        _qkv_kernel,
        grid=(rows // tm,),
        in_specs=[row_spec(D_MODEL), _resident((1, D_MODEL)), _layer_of(w_in, layer),
                  _resident((1, 2 * MIX_WIDTH)), _resident((MXU_DIM, MXU_DIM)), _resident((MXU_DIM, MXU_DIM))]
                 + [row_spec(LANES)] * 4,
        out_specs=[row_spec(3 * W_DIFF), row_spec(3 * W_DIL), row_spec(3 * W_MOBA)],
        out_shape=[jax.ShapeDtypeStruct((rows, 3 * W_DIFF), BF16),
                   jax.ShapeDtypeStruct((rows, 3 * W_DIL), F32),
                   jax.ShapeDtypeStruct((rows, 3 * W_MOBA), BF16)],
        compiler_params=_params(1),
        name="qkv_project",
    )(x2d, norm_w.reshape(1, D_MODEL), w_in, gains,
      _group_mean_matrix(DIFF_HALF), _group_mean_matrix(HEAD_DIM), *tables)


def _reduce_rows(x, pair_op, final_op):
    n = x.shape[0]
    while n > 8 and n % 2 == 0:
        n //= 2
        x = pair_op(x[:n], x[n:])
    return final_op(x, axis=0, keepdims=True)


ONES_ROWS = 16


def _store_scores(s_t, buf, cols):
    s_ref, smax_ref = buf
    s_ref[:, cols] = s_t
    smax_ref[:, cols] = _reduce_rows(s_t, jnp.maximum, jnp.max)


def _softmax_update_t(buf, v_ts, m_ref, l_ref, acc_ref, start=0, stop=None):
    s_ref, smax_ref = buf
    stop = s_ref.shape[1] if stop is None else stop
    lanes = slice(start, stop)
    width = (stop - start) // len(v_ts)
    m_old = m_ref[:, lanes]
    m_new = jnp.maximum(m_old, smax_ref[:, lanes])
    alpha = jnp.exp2(m_old - m_new)
    e = jnp.exp2(s_ref[:, lanes] - m_new).astype(BF16)
    for h, v_t in enumerate(v_ts):
        cols = slice(start + h * width, start + (h + 1) * width)
        rel = slice(h * width, (h + 1) * width)
        pv = _dot(v_t, e[:, rel])
        acc_ref[:, cols] = alpha[:, rel] * acc_ref[:, cols] + pv[0:HEAD_DIM]
        l_ref[:, cols] = alpha[:, rel] * l_ref[:, cols] + pv[HEAD_DIM:HEAD_DIM + 1]
    m_ref[:, lanes] = m_new


def _transpose_values(v_ref, vt_ref):
    n_head, n_chunk, _, tk = vt_ref.shape

    def one_chunk(c, carry):
        rows = pl.ds(pl.multiple_of(c * tk, tk), tk)
        for p in range(n_head // 2):
            pair_t = v_ref[0, rows, p * LANES:(p + 1) * LANES].astype(F32).T.astype(BF16)
            for h in range(2):
                vt_ref[2 * p + h, c, 0:HEAD_DIM, :] = pair_t[h * HEAD_DIM:(h + 1) * HEAD_DIM]
                vt_ref[2 * p + h, c, HEAD_DIM:HEAD_DIM + ONES_ROWS, :] = jnp.ones((ONES_ROWS, tk), BF16)
        return carry

    lax.fori_loop(0, n_chunk, one_chunk, 0)


def _diff_kernel(lq1_ref, lk1_ref, lq2_ref, lk2_ref, subln_ref, q_ref, k_ref, v_ref, o_ref,
                 vt_ref, q4t_ref, s0_ref, s1_ref, smax0_ref, smax1_ref, m_ref, l_ref, acc_ref, *, tq, tk, lam_init):
    i = pl.program_id(1)
    n_pair = vt_ref.shape[0] // 2
    width = 4 * tq
    pair_cols = [slice(p * width, (p + 1) * width) for p in range(n_pair)]
    pair_dims = [slice(p * LANES, (p + 1) * LANES) for p in range(n_pair)]

    @pl.when(i == 0)
    def _():
        _transpose_values(v_ref, vt_ref)

    dim = lax.broadcasted_iota(jnp.int32, (LANES, 1), 0)
    for p in range(n_pair):
        q_t = q_ref[0, :, pair_dims[p]].astype(F32).T
        for n in range(4):
            keep = (dim >= n * DIFF_HALF) & (dim < (n + 1) * DIFF_HALF)
            q4t_ref[:, p * width + n * tq:p * width + (n + 1) * tq] = jnp.where(keep, q_t, 0.0).astype(BF16)
    m_ref[...] = jnp.full(m_ref.shape, MASKED, F32)
    l_ref[...] = jnp.zeros(l_ref.shape, F32)
    acc_ref[...] = jnp.zeros(acc_ref.shape, F32)

    head_width = width // 2

    def scores(j, s_ref, masked, heads=range(2 * n_pair)):
        rows = pl.ds(pl.multiple_of(j * tk, tk), tk)
        if masked:
            key_pos = lax.broadcasted_iota(jnp.int32, (tk, head_width), 0) + j * tk
            query_pos = (lax.broadcasted_iota(jnp.int32, (tk, head_width), 1) & (tq - 1)) + i * tq
            causal = key_pos <= query_pos
        for h in heads:
            cols = slice(h * head_width, (h + 1) * head_width)
            s_t = _dot(k_ref[0, rows, pair_dims[h // 2]], q4t_ref[:, cols])
            _store_scores(jnp.where(causal, s_t, -jnp.inf) if masked else s_t, s_ref, cols)

    def update(j, s_ref, next_j=None, next_ref=None):
        for h in range(2 * n_pair):
            if next_ref is not None:
                scores(next_j, next_ref, False, heads=(h,))
            _softmax_update_t(s_ref, [vt_ref[h, j]], m_ref, l_ref, acc_ref, h * head_width, (h + 1) * head_width)

    s0_ref, s1_ref = (s0_ref, smax0_ref), (s1_ref, smax1_ref)

    n_full = (i * tq) // tk
    scores(n_full, s0_ref, True)

    def chunk_pair(u, carry):
        update(jnp.where(u == 0, n_full, 2 * u - 1), s0_ref, 2 * u, s1_ref)
        update(2 * u, s1_ref, 2 * u + 1, s0_ref)
        return carry

    n_pairs = n_full // 2
    lax.fori_loop(0, n_pairs, chunk_pair, 0)
    pending = jnp.where(n_pairs == 0, n_full, 2 * n_pairs - 1)

    @pl.when(n_full % 2 == 1)
    def _():
        update(pending, s0_ref, n_full - 1, s1_ref)
        update(n_full - 1, s1_ref)

    @pl.when(n_full % 2 == 0)
    def _():
        update(pending, s0_ref)

    lam = (jnp.exp(jnp.sum(lq1_ref[...] * lk1_ref[...], axis=-1, keepdims=True))
           - jnp.exp(jnp.sum(lq2_ref[...] * lk2_ref[...], axis=-1, keepdims=True)) + lam_init)
    for p in range(n_pair):
        o4 = acc_ref[:, pair_cols[p]] / l_ref[:, pair_cols[p]]
        heads = []
        for h in range(2):
            o = o4[:, 2 * h * tq:(2 * h + 1) * tq] - lam * o4[:, (2 * h + 1) * tq:(2 * h + 2) * tq]
            heads.append(o * lax.rsqrt(jnp.mean(o * o, axis=0, keepdims=True) + EPS))
        o = jnp.concatenate(heads, axis=0) * subln_ref[...] * (1.0 - lam_init)
        o_ref[0, :, pair_dims[p]] = o.T.astype(o_ref.dtype)


def _diff_attention(qkv, lam_vecs, subln, lam_init, tq=256, tk=512):
    batch, seq, _ = qkv.shape
    n_pair = W_DIFF // LANES
    vec = _resident((1, DIFF_HALF))
    kernel = functools.partial(_diff_kernel, tq=tq, tk=tk, lam_init=lam_init)
    lanes = n_pair * 4 * tq
    return pl.pallas_call(
        kernel,
        grid=(batch, seq // tq),
        in_specs=[vec, vec, vec, vec, _resident((LANES, 1)),
                  pl.BlockSpec((1, tq, W_DIFF), lambda b, i: (b, i, 0)),
                  pl.BlockSpec((1, seq, W_DIFF), lambda b, i: (b, 0, 1)),
                  pl.BlockSpec((1, seq, W_DIFF), lambda b, i: (b, 0, 2))],
        out_specs=pl.BlockSpec((1, tq, W_DIFF), lambda b, i: (b, i, 0)),
        out_shape=jax.ShapeDtypeStruct((batch, seq, W_DIFF), BF16),
        scratch_shapes=[pltpu.VMEM((2 * n_pair, seq // tk, HEAD_DIM + ONES_ROWS, tk), BF16),
                        pltpu.VMEM((LANES, lanes), BF16),
                        pltpu.VMEM((tk, lanes), F32), pltpu.VMEM((tk, lanes), F32),
                        pltpu.VMEM((1, lanes), F32), pltpu.VMEM((1, lanes), F32),
                        pltpu.VMEM((1, lanes), F32), pltpu.VMEM((1, lanes), F32),
                        pltpu.VMEM((HEAD_DIM, lanes), F32)],
        compiler_params=_params(2),
        name="diff_attention",
    )(*[v.reshape(1, DIFF_HALF) for v in lam_vecs], jnp.tile(subln, 2).reshape(LANES, 1), qkv, qkv, qkv)


def _dil_kernel(q_ref, k_ref, v_ref, o_ref, o0_ref, o1_ref, o2_ref, l0_ref, l1_ref, l2_ref, k4_ref, v4_ref,
                *, tile, group):
    base = pl.program_id(2) * tile
    w = DIL_STEPS
    lane = _lane_iota()
    seq = k_ref.shape[1]
    quarter = seq // 4

    @pl.when(pl.program_id(2) == 0)
    def _():
        for r in range(4):
            k4_ref[r * quarter:(r + 1) * quarter, :] = k_ref[0, pl.ds(r, quarter, stride=4), :]
            v4_ref[r * quarter:(r + 1) * quarter, :] = v_ref[0, pl.ds(r, quarter, stride=4), :]
    qi = lax.broadcasted_iota(jnp.int32, (2 * w, 2 * w), 0) & (w - 1)
    kj = lax.broadcasted_iota(jnp.int32, (2 * w, 2 * w), 1)
    band_bias = jnp.where((kj >= qi) & (kj <= qi + w), 0.0, -jnp.inf)

    for (window, d), ob_ref, lse_ref in zip(DILATED_PAIRS, (o0_ref, o1_ref, o2_ref), (l0_ref, l1_ref, l2_ref)):
        def rows(ref, start, d=d):
            if d == 1:
                return ref[0, pl.ds(start, w), :]
            if d > 4 and d % 4 == 0 and (ref is k_ref or ref is v_ref):
                ref4 = k4_ref if ref is k_ref else v4_ref
                return ref4[pl.ds((start & 3) * quarter + (start >> 2), w, stride=d // 4), :]
            return ref[0, pl.ds(start, w, stride=d), :]

        def block_group(grp, carry, window=window, d=d, ob_ref=ob_ref, lse_ref=lse_ref, rows=rows):
            local, prev, own, no_prev_bias, s, e, m, o = {}, {}, {}, {}, {}, {}, {}, {}

            def stage_scores(g):
                blk = grp * group + g
                local[g] = (blk // d) * window + (blk & (d - 1))
                own[g] = base + local[g]
                has_prev = own[g] >= window
                prev[g] = jnp.where(has_prev, own[g] - window, own[g])
                no_prev_bias[g] = jnp.where(has_prev, 0.0, -jnp.inf)
                q = rows(q_ref, local[g])
                q2 = jnp.concatenate([jnp.where(lane < HEAD_DIM, q, 0.0), jnp.where(lane < HEAD_DIM, 0.0, q)], axis=0)
                k2 = jnp.concatenate([rows(k_ref, prev[g]), rows(k_ref, own[g])], axis=0)
                s[g] = _dot_nt(q2.astype(BF16), k2.astype(BF16))

            def stage_softmax(g):
                sg = s.pop(g) + band_bias
                sg = jnp.concatenate([sg[:, 0:w] + no_prev_bias[g], sg[:, w:2 * w]], axis=1)
                m[g] = jnp.max(sg, axis=-1, keepdims=True)
                e[g] = jnp.exp2(sg - m[g]).astype(BF16)

            def stage_values(g):
                v2 = jnp.concatenate([rows(v_ref, prev[g]), rows(v_ref, own[g])], axis=0).astype(BF16)
                o[g] = _dot(e.pop(g), jnp.concatenate([v2, jnp.ones_like(v2)], axis=1))

            def stage_output(g):
                first = lane < HEAD_DIM
                og = o.pop(g)
                l = jnp.where(first, og[0:w, LANES:2 * LANES], og[w:2 * w, LANES:2 * LANES])
                ob = jnp.where(first, og[0:w, 0:LANES], og[w:2 * w, 0:LANES]) / l
                lb = jnp.where(first, m[g][0:w], m[g][w:2 * w]) + jnp.log2(l)
                if d == 1:
                    ob_ref[pl.ds(local[g], w), :] = ob
                    lse_ref[pl.ds(local[g], w), :] = lb
                else:
                    ob_ref[pl.ds(local[g], w, stride=d), :] = ob
                    lse_ref[pl.ds(local[g], w, stride=d), :] = lb

            stages = (stage_scores, stage_softmax, stage_values, stage_output)
            for step in range(group + len(stages) - 1):
                for lag, stage in enumerate(stages):
                    if 0 <= step - lag < group:
                        stage(step - lag)
            return carry

        lax.fori_loop(0, tile // w // group, block_group, 0)

    lse = [l0_ref[...], l1_ref[...], l2_ref[...]]
    m = jnp.maximum(jnp.maximum(lse[0], lse[1]), lse[2])
    wts = [jnp.exp2(x - m) for x in lse]
    mixed = wts[0] * o0_ref[...] + wts[1] * o1_ref[...] + wts[2] * o2_ref[...]
    o_ref[0] = (mixed / (wts[0] + wts[1] + wts[2])).astype(o_ref.dtype)


def _dilated_attention(qkv, group=16):
    batch, seq, _ = qkv.shape
    n_pairs = W_DIL // LANES
    tile = max(window for window, _ in DILATED_PAIRS)
    assert seq % tile == 0
    return pl.pallas_call(
        functools.partial(_dil_kernel, tile=tile, group=group),
        grid=(batch, n_pairs, seq // tile),
        in_specs=[pl.BlockSpec((1, tile, LANES), lambda b, p, i: (b, i, p)),
                  pl.BlockSpec((1, seq, LANES), lambda b, p, i: (b, 0, n_pairs + p)),
                  pl.BlockSpec((1, seq, LANES), lambda b, p, i: (b, 0, 2 * n_pairs + p))],
        out_specs=pl.BlockSpec((1, tile, LANES), lambda b, p, i: (b, i, p)),
        out_shape=jax.ShapeDtypeStruct((batch, seq, W_DIL), BF16),
        scratch_shapes=[pltpu.VMEM((tile, LANES), F32)] * 6 + [pltpu.VMEM((seq, LANES), F32)] * 2,
        compiler_params=_params(3),
        name="dilated_attention",
    )(qkv, qkv, qkv)


def _moba_kernel(q_ref, k_ref, v_ref, o_ref, kaug_ref, vt_ref, kmh_ref, kml_ref, qaugt_ref, so_ref, s0_ref, s1_ref,
                 smaxo_ref, smax0_ref, smax1_ref, m_ref, l_ref, acc_ref):
    i = pl.program_id(1)
    blk = MOBA_BLOCK
    tk = 2 * blk
    blk_shift = blk.bit_length() - 1
    seq = k_ref.shape[1]
    n_blk = seq // blk
    n_pair = vt_ref.shape[0] // 2
    width = 2 * blk
    pair_cols = [slice(p * width, (p + 1) * width) for p in range(n_pair)]
    pair_dims = [slice(p * LANES, (p + 1) * LANES) for p in range(n_pair)]

    @pl.when(i == 0)
    def _():
        qaugt_ref[LANES:2 * LANES, :] = jnp.zeros((LANES, n_pair * width), BF16)
        _transpose_values(v_ref, vt_ref)
        piece = 1024
        for p in range(n_pair):
            def one_piece(c, total, p=p):
                rows = pl.ds(pl.multiple_of(c * piece, piece), piece)
                kc = k_ref[0, rows, pair_dims[p]]
                key_blk = (lax.broadcasted_iota(jnp.int32, (piece, LANES), 0) + c * piece) >> blk_shift
                kaug_ref[p, rows, 0:LANES] = kc
                kaug_ref[p, rows, LANES:2 * LANES] = (
                    lax.broadcasted_iota(jnp.int32, (piece, LANES), 1) == key_blk).astype(BF16)
                member = (lax.broadcasted_iota(jnp.int32, (LANES, piece), 0)
                          == (lax.broadcasted_iota(jnp.int32, (LANES, piece), 1) + c * piece) >> blk_shift)
                return total + _dot(member.astype(BF16), kc)

            total = lax.fori_loop(0, seq // piece, one_piece, jnp.zeros((LANES, LANES), F32))
            k_mean = total * (1.0 / blk)
            hi = k_mean.astype(BF16)
            kmh_ref[p] = hi
            kml_ref[p] = (k_mean - hi.astype(F32)).astype(BF16)

    dim = lax.broadcasted_iota(jnp.int32, (LANES, 1), 0)
    q2t, gates = [], []
    for p in range(n_pair):
        q_t = q_ref[0, :, pair_dims[p]].astype(F32).T
        q2t.append(jnp.concatenate([jnp.where(dim < HEAD_DIM, q_t, 0.0), jnp.where(dim < HEAD_DIM, 0.0, q_t)],
                                   axis=1).astype(BF16))
        gates.append(_dot(kmh_ref[p, 0:n_blk, :], q2t[p]) + _dot(kml_ref[p, 0:n_blk, :], q2t[p]))

    gate = jnp.concatenate(gates, axis=1)
    blk_id = lax.broadcasted_iota(jnp.int32, gate.shape, 0)
    blk_f = blk_id.astype(F32)
    g = jnp.where(blk_id < i, gate, -jnp.inf)
    selected = jnp.zeros(gate.shape, F32)
    for _ in range(MOBA_TOPK):
        best = jnp.max(g, axis=0, keepdims=True)
        first = jnp.min(jnp.where(g == best, blk_f, float(LANES)), axis=0, keepdims=True)
        pick = (blk_f == first) & (best > -jnp.inf)
        selected = jnp.where(pick, 1.0, selected)
        g = jnp.where(pick, -jnp.inf, g)
    for p in range(n_pair):
        qaugt_ref[0:LANES, pair_cols[p]] = q2t[p]
    qaugt_ref[LANES:LANES + n_blk, :] = jnp.where(selected > 0.0, 0.0, MASKED).astype(BF16)

    m_ref[...] = jnp.full(m_ref.shape, MASKED, F32)
    l_ref[...] = jnp.zeros(l_ref.shape, F32)
    acc_ref[...] = jnp.zeros(acc_ref.shape, F32)

    def scores(c, s_ref, pairs=range(n_pair)):
        rows = pl.ds(pl.multiple_of(c * tk, tk), tk)
        for p in pairs:
            _store_scores(_dot(kaug_ref[p, rows, :], qaugt_ref[:, pair_cols[p]]), s_ref, pair_cols[p])

    def update(c, s_ref, next_c=None, next_ref=None):
        for p in range(n_pair):
            if next_ref is not None:
                scores(next_c, next_ref, pairs=(p,))
            v_ts = [jnp.concatenate([vt_ref[2 * p + h, 2 * c], vt_ref[2 * p + h, 2 * c + 1]], axis=1) for h in range(2)]
            _softmax_update_t(s_ref, v_ts, m_ref, l_ref, acc_ref, p * width, (p + 1) * width)

    so_ref, s0_ref, s1_ref = (so_ref, smaxo_ref), (s0_ref, smax0_ref), (s1_ref, smax1_ref)

    own_rows = pl.ds(pl.multiple_of(i * blk, blk), blk)
    key = lax.broadcasted_iota(jnp.int32, (blk, width), 0)
    query = lax.broadcasted_iota(jnp.int32, (blk, width), 1) & (blk - 1)
    for p in range(n_pair):
        s_own = _dot(kaug_ref[p, own_rows, 0:LANES], q2t[p])
        _store_scores(jnp.where(key <= query, s_own, -jnp.inf), so_ref, pair_cols[p])
    for p in range(n_pair):
        scores(0, s0_ref, pairs=(p,))
        _softmax_update_t(so_ref, [vt_ref[2 * p + h, i] for h in range(2)], m_ref, l_ref, acc_ref,
                          p * width, (p + 1) * width)

    n = jnp.maximum((i + 1) // 2, 1)

    def chunk_pair(u, carry):
        update(2 * u, s0_ref, 2 * u + 1, s1_ref)
        update(2 * u + 1, s1_ref, 2 * u + 2, s0_ref)
        return carry

    lax.fori_loop(0, (n - 1) // 2, chunk_pair, 0)

    @pl.when(n % 2 == 1)
    def _():
        update(n - 1, s0_ref)

    @pl.when(n % 2 == 0)
    def _():
        update(n - 2, s0_ref, n - 1, s1_ref)
        update(n - 1, s1_ref)

    for p in range(n_pair):
        o = acc_ref[:, pair_cols[p]] / l_ref[:, pair_cols[p]]
        o_ref[0, :, pair_dims[p]] = jnp.concatenate([o[:, 0:blk], o[:, blk:2 * blk]], axis=0).T.astype(o_ref.dtype)


def _moba_attention(qkv):
    batch, seq, _ = qkv.shape
    n_pair = W_MOBA // LANES
    blk = MOBA_BLOCK
    lanes = n_pair * 2 * blk
    assert seq % 1024 == 0 and seq // blk <= LANES and seq // blk >= MOBA_TOPK
    assert (seq // blk) % 16 == 0
    whole = lambda j: pl.BlockSpec((1, seq, W_MOBA), lambda b, i: (b, 0, j), pipeline_mode=pl.Buffered(1))
    return pl.pallas_call(
        _moba_kernel,
        grid=(batch, seq // blk),
        in_specs=[pl.BlockSpec((1, blk, W_MOBA), lambda b, i: (b, i, 0)), whole(1), whole(2)],
        out_specs=pl.BlockSpec((1, blk, W_MOBA), lambda b, i: (b, i, 0)),
        out_shape=jax.ShapeDtypeStruct((batch, seq, W_MOBA), BF16),
        scratch_shapes=[pltpu.VMEM((n_pair, seq, 2 * LANES), BF16),
                        pltpu.VMEM((2 * n_pair, seq // blk, HEAD_DIM + ONES_ROWS, blk), BF16),
                        pltpu.VMEM((n_pair, LANES, LANES), BF16), pltpu.VMEM((n_pair, LANES, LANES), BF16),
                        pltpu.VMEM((2 * LANES, lanes), BF16), pltpu.VMEM((blk, lanes), F32),
                        pltpu.VMEM((2 * blk, lanes), F32), pltpu.VMEM((2 * blk, lanes), F32),
                        pltpu.VMEM((1, lanes), F32), pltpu.VMEM((1, lanes), F32), pltpu.VMEM((1, lanes), F32),
                        pltpu.VMEM((1, lanes), F32), pltpu.VMEM((1, lanes), F32),
                        pltpu.VMEM((HEAD_DIM, lanes), F32)],
        compiler_params=_params(2),
        name="moba_attention",
    )(qkv, qkv, qkv)


def _mem_kv_kernel(mem_ref, nw_ref, w_ref, g64_ref, gain_ref, k_ref, v_ref):
    h = _rms(mem_ref[0], nw_ref[...]).astype(BF16)
    kv = _dot(h, w_ref[...])
    k = kv[:, 0:MEM_WIDTH]
    ms = _dot((k * k).astype(BF16), g64_ref[...])
    k_ref[0] = (k * lax.rsqrt(ms + EPS) * gain_ref[...]).astype(BF16)
    v_ref[0] = kv[:, MEM_WIDTH:2 * MEM_WIDTH].astype(BF16)


def _mem_kv(mem, norm_w, w_mkv, layer, k_gain):
    batch, mem_len, _ = mem.shape
    out = jax.ShapeDtypeStruct((batch, mem_len, MEM_WIDTH), BF16)
    return pl.pallas_call(
        _mem_kv_kernel,
        grid=(batch,),
        in_specs=[pl.BlockSpec((1, mem_len, D_MODEL), lambda b: (b, 0, 0)), _resident((1, D_MODEL)),
                  _layer_of(w_mkv, layer), _resident((MXU_DIM, MXU_DIM)), _resident((1, MEM_WIDTH))],
        out_specs=[pl.BlockSpec((1, mem_len, MEM_WIDTH), lambda b: (b, 0, 0))] * 2,
        out_shape=[out, out],
        compiler_params=_params(1),
        name="mem_kv",
    )(mem, norm_w.reshape(1, D_MODEL), w_mkv, _group_mean_matrix(HEAD_DIM),
      jnp.tile(k_gain, N_HEADS_MEM).reshape(1, MEM_WIDTH))


def _post_kernel(x_ref, a_ref, b_ref, c_ref, wo_ref, ncross_ref, wmq_ref, g64_ref, qgain_ref, km_ref, vm_ref,
                 wmo_ref, nffn_ref, wgu_ref, wd_ref, o_ref, *, tm):
    mix = jnp.concatenate([a_ref[...], b_ref[...], c_ref[...]], axis=1)
    x = x_ref[...] + _dot(mix, wo_ref[...])

    q = _dot(_rms(x, ncross_ref[...]).astype(BF16), wmq_ref[...])
    ms = _dot((q * q).astype(BF16), g64_ref[...])
    qn = (q * lax.rsqrt(ms + EPS) * qgain_ref[...]).astype(BF16)
    head = _lane_iota(MEM_WIDTH) >> (HEAD_DIM.bit_length() - 1)
    q4 = jnp.concatenate([jnp.where(head == n, qn, jnp.zeros_like(qn)) for n in range(N_HEADS_MEM)], axis=0)
    s = _dot_nt(q4, km_ref[0])
    e = jnp.exp(s - jnp.max(s, axis=-1, keepdims=True))
    o4 = _dot(e.astype(BF16), vm_ref[0]) / jnp.sum(e, axis=-1, keepdims=True)
    o = jnp.where(head == 0, o4[0:tm], 0.0)
    for n in range(1, N_HEADS_MEM):
        o = o + jnp.where(head == n, o4[n * tm:(n + 1) * tm], 0.0)
    x = x + _dot(o.astype(BF16), wmo_ref[...])

    h = _rms(x, nffn_ref[...]).astype(BF16)
    y = jnp.zeros((tm, D_MODEL), F32)
    for c in range(D_FF // MXU_DIM):
        lo = c * MXU_DIM
        g = _dot(h, wgu_ref[:, lo:lo + MXU_DIM])
        u = _dot(h, wgu_ref[:, D_FF + lo:D_FF + lo + MXU_DIM])
        y = y + _dot((g * jax.nn.sigmoid(g) * u).astype(BF16), wd_ref[lo:lo + MXU_DIM, :])
    o_ref[...] = x + y


def _post_attention(x2d, outs, seq, layer, w_out, norm_cross, w_mq, q_gain, k_mem, v_mem, w_mo, norm_ffn,
                    w_gate_up, w_down, tm=512):
    rows = x2d.shape[0]
    mem_len = k_mem.shape[1]
    tiles_per_seq = seq // tm
    row_spec = lambda width: pl.BlockSpec((tm, width), lambda i: (i, 0))
    mem_spec = pl.BlockSpec((1, mem_len, MEM_WIDTH), lambda i: (i // tiles_per_seq, 0, 0))
    scale = HEAD_DIM ** -0.5
    return pl.pallas_call(
        functools.partial(_post_kernel, tm=tm),
        grid=(rows // tm,),
        in_specs=[row_spec(D_MODEL), row_spec(W_DIFF), row_spec(W_DIL), row_spec(W_MOBA),
                  _layer_of(w_out, layer), _resident((1, D_MODEL)), _layer_of(w_mq, layer),
                  _resident((MXU_DIM, MXU_DIM)), _resident((1, MEM_WIDTH)), mem_spec, mem_spec,
                  _layer_of(w_mo, layer), _resident((1, D_MODEL)), _layer_of(w_gate_up, layer),
                  _layer_of(w_down, layer)],
        out_specs=row_spec(D_MODEL),
        out_shape=jax.ShapeDtypeStruct((rows, D_MODEL), F32),
        compiler_params=_params(1),
        name="post_attention",
    )(x2d, *[o.reshape(rows, -1) for o in outs], w_out, norm_cross.reshape(1, D_MODEL),
      w_mq, _group_mean_matrix(HEAD_DIM), (jnp.tile(q_gain, N_HEADS_MEM) * scale).reshape(1, MEM_WIDTH),
      k_mem, v_mem, w_mo, norm_ffn.reshape(1, D_MODEL), w_gate_up, w_down)


def kernel(x, mem, positions, norm_mix, w_in, qn_diff, kn_diff, lambda_q1, lambda_k1, lambda_q2, lambda_k2,
           subln_diff, qn_dil, kn_dil, qn_moba, kn_moba, w_out, norm_cross, norm_mem, w_mq, w_mkv, qn_mem,
           kn_mem, w_mo, norm_ffn, w_gate_up, w_down):
    batch, seq, _ = x.shape
    depth = w_in.shape[0]
    rows = batch * seq
    tables = _rope_tables(positions)
    x2d = x.reshape(rows, D_MODEL)
    w_in, w_out, w_mq, w_mkv, w_mo, w_gate_up, w_down = (
        w.astype(BF16) for w in (w_in, w_out, w_mq, w_mkv, w_mo, w_gate_up, w_down))
    for l in range(depth):
        gains = jnp.concatenate([
            jnp.tile(qn_diff[l], 2 * N_HEADS_DIFF) * (DIFF_HALF ** -0.5 * LOG2_E),
            jnp.tile(qn_dil[l], N_HEADS_DIL) * (HEAD_DIM ** -0.5 * LOG2_E),
            jnp.tile(qn_moba[l], N_HEADS_MOBA) * (HEAD_DIM ** -0.5 * LOG2_E),
            jnp.tile(kn_diff[l], 2 * N_HEADS_DIFF), jnp.tile(kn_dil[l], N_HEADS_DIL),
            jnp.tile(kn_moba[l], N_HEADS_MOBA)]).reshape(1, 2 * MIX_WIDTH)
        qkv_a, qkv_b, qkv_c = _qkv_project(x2d, norm_mix[l], w_in, l, gains, tables)
        lam_init = 0.8 - 0.6 * math.exp(-0.3 * l)
        o_a = _diff_attention(qkv_a.reshape(batch, seq, -1),
                              (lambda_q1[l], lambda_k1[l], lambda_q2[l], lambda_k2[l]), subln_diff[l], lam_init)
        o_b = _dilated_attention(qkv_b.reshape(batch, seq, -1))
        o_c = _moba_attention(qkv_c.reshape(batch, seq, -1))
        k_mem, v_mem = _mem_kv(mem, norm_mem[l], w_mkv, l, kn_mem[l])
        x2d = _post_attention(x2d, (o_a, o_b, o_c), seq, l, w_out, norm_cross[l], w_mq, qn_mem[l], k_mem, v_mem,
                              w_mo, norm_ffn[l], w_gate_up, w_down)
    return x2d.reshape(batch, seq, D_MODEL)
```
